```python
import jax, jax.numpy as jnp
from jax import lax
import numpy as np

D_MODEL = 2048
BATCH = 2
SEQ = 8192
DEPTH = 4

N_META = 16
FNET_HEADS = 4
FNET_HEAD_DIM = D_MODEL // 8
FNET_WIDTH = FNET_HEADS * FNET_HEAD_DIM
CONV_HEADS = 4
CONV_HEAD_DIM = D_MODEL // 8
CONV_WIDTH = CONV_HEADS * CONV_HEAD_DIM
CONV_KERNEL = 31
SC_WIDTH = D_MODEL
SC_KERNEL = 3
D_FF = 4 * D_MODEL
N_EVEN = (DEPTH + 1) // 2
N_ODD = DEPTH // 2
AB_IN = FNET_WIDTH + 2 * CONV_WIDTH
AB_OUT = FNET_WIDTH + CONV_WIDTH
EPS = 1e-6

kernel_name = "hybrid_fnet_conformer_shortconv_encoder"


def rms_norm(x, g):
    xf = x.astype(jnp.float32)
    y = xf * lax.rsqrt(jnp.mean(xf * xf, axis=-1, keepdims=True) + EPS)
    return (y * g.astype(jnp.float32)).astype(x.dtype)


def depthwise_conv_centred(h, w):
    k = w.shape[0]
    pad = (k - 1) // 2
    return lax.conv_general_dilated(
        h, w[:, None, :].astype(h.dtype), window_strides=(1,),
        padding=[(pad, pad)], dimension_numbers=("NWC", "WIO", "NWC"),
        feature_group_count=h.shape[-1])


def fourier_mix(u):
    f = jnp.fft.fft2(u.astype(jnp.float32), axes=(1, 3), norm="ortho")
    return jnp.real(f).astype(u.dtype)


def conformer_conv(v, conv_w, conv_b, ln_g, ln_b):
    a, gate = jnp.split(v, 2, axis=-1)
    h = a * jax.nn.sigmoid(gate)
    h = depthwise_conv_centred(h, conv_w) + conv_b.astype(h.dtype)
    b_, l_, _ = h.shape
    hg = h.reshape(b_, l_, CONV_HEADS, CONV_HEAD_DIM).astype(jnp.float32)
    mu = jnp.mean(hg, axis=-1, keepdims=True)
    var = jnp.mean(jnp.square(hg - mu), axis=-1, keepdims=True)
    hg = ((hg - mu) * lax.rsqrt(var + EPS)).reshape(b_, l_, CONV_WIDTH)
    hg = hg * ln_g.astype(jnp.float32) + ln_b.astype(jnp.float32)
    return jax.nn.silu(hg).astype(v.dtype)


def even_mixer(xn, w_in, w_out, conv_w, conv_b, ln_g, ln_b):
    b_, l_, _ = xn.shape
    proj = xn @ w_in
    u = proj[..., :FNET_WIDTH]
    v = proj[..., FNET_WIDTH:]
    ya = fourier_mix(u.reshape(b_, l_, FNET_HEADS, FNET_HEAD_DIM)).reshape(b_, l_, FNET_WIDTH)
    yb = conformer_conv(v, conv_w, conv_b, ln_g, ln_b)
    return jnp.concatenate([ya, yb], axis=-1) @ w_out


def odd_mixer(xn, w_in, conv_w, w_out):
    gb, gc, hin = jnp.split(xn @ w_in, 3, axis=-1)
    y = gb * depthwise_conv_centred(gc * hin, conv_w)
    return y @ w_out


def sq_relu_mlp(xn, w1, w2):
    return jnp.square(jax.nn.relu(xn @ w1)) @ w2


def setup_inputs(seed: int = 0) -> dict:
    key = jax.random.key(seed)
    ks = jax.random.split(key, 20)
    nrm = jax.random.normal
    f32 = jnp.float32
    d = D_MODEL
    return {
        "x": nrm(ks[0], (BATCH, SEQ, d), f32),
        "meta_tokens": nrm(ks[1], (N_META, d), f32),
        "norm_mix_g": 1.0 + 0.02 * nrm(ks[2], (DEPTH, d), f32),
        "norm_mlp_g": 1.0 + 0.02 * nrm(ks[3], (DEPTH, d), f32),
        "norm_final_g": 1.0 + 0.02 * nrm(ks[4], (d,), f32),
        "ab_w_in": nrm(ks[5], (N_EVEN, d, AB_IN), f32) * d ** -0.5,
        "ab_w_out": nrm(ks[6], (N_EVEN, AB_OUT, d), f32) * AB_OUT ** -0.5,
        "ab_conv_w": nrm(ks[7], (N_EVEN, CONV_KERNEL, CONV_WIDTH), f32) * CONV_KERNEL ** -0.5,
        "ab_conv_b": 0.01 * nrm(ks[8], (N_EVEN, CONV_WIDTH), f32),
        "ab_ln_g": 1.0 + 0.02 * nrm(ks[9], (N_EVEN, CONV_WIDTH), f32),
        "ab_ln_b": 0.01 * nrm(ks[10], (N_EVEN, CONV_WIDTH), f32),
        "c_w_in": nrm(ks[11], (N_ODD, d, 3 * SC_WIDTH), f32) * d ** -0.5,
        "c_conv_w": nrm(ks[12], (N_ODD, SC_KERNEL, SC_WIDTH), f32) * SC_KERNEL ** -0.5,
        "c_w_out": nrm(ks[13], (N_ODD, SC_WIDTH, d), f32) * SC_WIDTH ** -0.5,
        "mlp_w1": nrm(ks[14], (DEPTH, d, D_FF), f32) * d ** -0.5,
        "mlp_w2": nrm(ks[15], (DEPTH, D_FF, d), f32) * D_FF ** -0.5,
    }


def reference(x, meta_tokens, norm_mix_g, norm_mlp_g, norm_final_g,
              ab_w_in, ab_w_out, ab_conv_w, ab_conv_b, ab_ln_g, ab_ln_b,
              c_w_in, c_conv_w, c_w_out, mlp_w1, mlp_w2):
    b_ = x.shape[0]
    meta = jnp.broadcast_to(meta_tokens[None].astype(x.dtype), (b_, N_META, D_MODEL))
    h = jnp.concatenate([meta, x], axis=1)
    for layer in range(DEPTH):
        hn = rms_norm(h, norm_mix_g[layer])
        i = layer // 2
        if layer % 2 == 0:
            h = h + even_mixer(hn, ab_w_in[i], ab_w_out[i], ab_conv_w[i],
                               ab_conv_b[i], ab_ln_g[i], ab_ln_b[i])
        else:
            h = h + odd_mixer(hn, c_w_in[i], c_conv_w[i], c_w_out[i])
        h = h + sq_relu_mlp(rms_norm(h, norm_mlp_g[layer]), mlp_w1[layer], mlp_w2[layer])
    h = rms_norm(h, norm_final_g)
    return h[:, N_META:, :]
```

```python
import functools
import math

import jax
import jax.numpy as jnp
from jax import lax
from jax.experimental import pallas as pl
from jax.experimental.pallas import tpu as pltpu

F32 = jnp.float32
BF16 = jnp.bfloat16

D_MODEL = 2048
BATCH = 2
SEQ = 8192
N_META = 16
SEQ_LEN = SEQ + N_META
ROWS = BATCH * SEQ_LEN
GROUPS = 4
GROUP_DIM = 256
MIX_WIDTH = GROUPS * GROUP_DIM
CONV_KERNEL = 31
CONV_PAD = (CONV_KERNEL - 1) // 2
D_FF = 4 * D_MODEL
EPS = 1e-6

RADIX = 4
DFT_LEN = SEQ_LEN // RADIX
DFT_ROWS = 2064
DFT_K = 2304
BFLY_ROWS = DFT_K // 3

TM_IN = 912
TM_OUT = 432
TM_MLP = 608
TF_MLP = 512
ODD_CHUNK = 512
HALO_EVEN = 16
HALO_ODD = 8
CONV_RC = 48
SHIFT_ROWS = TM_OUT + 2 * HALO_EVEN - 8
VMEM_LIMIT = 58 * 1024 * 1024

assert SEQ_LEN % TM_IN == 0 and SEQ_LEN % TM_OUT == 0 and ROWS % TM_MLP == 0
assert TM_OUT % CONV_RC == 0 and TM_OUT % HALO_EVEN == 0 and CONV_RC % 16 == 0


def _params(sem):
    return pltpu.CompilerParams(dimension_semantics=sem, vmem_limit_bytes=VMEM_LIMIT)


def _rms_gain(x, g):
    ms = jnp.mean(x * x, axis=-1, keepdims=True)
    return x * lax.rsqrt(ms + EPS) * g


def _resident(shape):
    nd = len(shape)
    return pl.BlockSpec(shape, lambda *_: (0,) * nd, pipeline_mode=pl.Buffered(1))


def _norm_kernel(h_ref, g_ref, o_ref):
    o_ref[...] = _rms_gain(h_ref[...], g_ref[...]).astype(o_ref.dtype)


def _norm_call(h, g):
    return pl.pallas_call(
        _norm_kernel,
        grid=(ROWS // TM_IN,),
        in_specs=[pl.BlockSpec((TM_IN, D_MODEL), lambda i: (i, 0)),
                  pl.BlockSpec((1, D_MODEL), lambda i: (0, 0))],
        out_specs=pl.BlockSpec((TM_IN, D_MODEL), lambda i: (i, 0)),
        out_shape=jax.ShapeDtypeStruct((ROWS, D_MODEL), BF16),
        compiler_params=_params(("parallel",)),
        name="prologue_norm",
    )(h, g)


def _even_in_kernel(xn_ref, w_ref, cs_ref, uc_ref, us_ref, glu_ref):
    proj = jnp.dot(xn_ref[...], w_ref[...], preferred_element_type=F32)
    u = proj[:, :GROUP_DIM].astype(BF16)
    cs = jnp.dot(u, cs_ref[...], preferred_element_type=F32)
    uc_ref[...] = cs[:, :GROUP_DIM].astype(BF16)
    us_ref[...] = cs[:, GROUP_DIM:].astype(BF16)
    a = proj[:, GROUP_DIM:2 * GROUP_DIM]
    gate = proj[:, 2 * GROUP_DIM:]
    glu_ref[...] = a * jax.nn.sigmoid(gate)


def _even_in_call(xn, w, cs):
    grp = jax.ShapeDtypeStruct((GROUPS, ROWS, GROUP_DIM), BF16)
    return pl.pallas_call(
        _even_in_kernel,
        grid=(ROWS // TM_IN, GROUPS),
        in_specs=[pl.BlockSpec((TM_IN, D_MODEL), lambda i, g: (i, 0)),
                  pl.BlockSpec((None, D_MODEL, 3 * GROUP_DIM), lambda i, g: (g, 0, 0)),
                  pl.BlockSpec((GROUP_DIM, 2 * GROUP_DIM), lambda i, g: (0, 0))],
        out_specs=[pl.BlockSpec((None, TM_IN, GROUP_DIM), lambda i, g: (g, i, 0)),
                   pl.BlockSpec((None, TM_IN, GROUP_DIM), lambda i, g: (g, i, 0)),
                   pl.BlockSpec((TM_IN, GROUP_DIM), lambda i, g: (i, g))],
        out_shape=[grp, grp, jax.ShapeDtypeStruct((ROWS, MIX_WIDTH), F32)],
        compiler_params=_params(("parallel", "arbitrary")),
        name="even_in_proj",
    )(xn, w, cs)


def _butterfly_kernel(uc_ref, us_ref, tw_ref, p_ref, q_ref):
    kb = pl.program_id(2)

    def chunk(c, carry):
        r0 = pl.multiple_of(c * 16, 16)
        rows = pl.ds(r0, 16)
        a = [uc_ref[k, rows, :].astype(F32) for k in range(RADIX)]
        b = [us_ref[k, rows, :].astype(F32) for k in range(RADIX)]
        tw = tw_ref[rows, :]
        row_id = kb * BFLY_ROWS + r0 + lax.broadcasted_iota(jnp.int32, (16, 1), 0)
        valid = row_id < DFT_LEN
        sa02, da02 = a[0] + a[2], a[0] - a[2]
        sa13, da13 = a[1] + a[3], a[1] - a[3]
        sb02, db02 = b[0] + b[2], b[0] - b[2]
        sb13, db13 = b[1] + b[3], b[1] - b[3]
        xs = [sa02 + sa13, da02 - db13, sa02 - sa13, da02 + db13]
        ys = [-(sb02 + sb13), -(db02 + da13), sb13 - sb02, da13 - db02]
        for r in range(RADIX):
            if r == 0:
                p, q = xs[0], ys[0]
            else:
                cr = tw[:, 2 * r - 2:2 * r - 1]
                sr = tw[:, 2 * r - 1:2 * r]
                p = cr * xs[r] + sr * ys[r]
                q = cr * ys[r] - sr * xs[r]
            lanes = slice(r * GROUP_DIM, (r + 1) * GROUP_DIM)
            p_ref[rows, lanes] = jnp.where(valid, p, 0.0).astype(BF16)
            q_ref[rows, lanes] = jnp.where(valid, q, 0.0).astype(BF16)
        return carry

    lax.fori_loop(0, BFLY_ROWS // 16, chunk, 0)


def _butterfly_call(uc, us, tw):
    in_spec = pl.BlockSpec((None, None, RADIX, BFLY_ROWS, GROUP_DIM),
                           lambda g, b, k: (g, b, 0, k, 0))
    out_spec = pl.BlockSpec((None, None, BFLY_ROWS, MIX_WIDTH), lambda g, b, k: (g, b, k, 0))
    out = jax.ShapeDtypeStruct((GROUPS, BATCH, DFT_K, MIX_WIDTH), BF16)
    return pl.pallas_call(
        _butterfly_kernel,
        grid=(GROUPS, BATCH, DFT_K // BFLY_ROWS),
        in_specs=[in_spec, in_spec, pl.BlockSpec((BFLY_ROWS, 8), lambda g, b, k: (k, 0))],
        out_specs=[out_spec, out_spec],
        out_shape=[out, out],
        compiler_params=_params(("parallel", "parallel", "parallel")),
        name="dft_butterfly",
    )(uc, us, tw)


def _dft_kernel(c_ref, s_ref, p_ref, q_ref, o_ref, *, scale):
    acc = jnp.dot(c_ref[...], p_ref[...], preferred_element_type=F32)
    acc = acc + jnp.dot(s_ref[...], q_ref[...], preferred_element_type=F32)
    o_ref[...] = (acc * scale).astype(o_ref.dtype)


def _dft_call(cm, sm, p, q):
    in_spec = pl.BlockSpec((None, None, DFT_K, GROUP_DIM), lambda g, b, r: (g, b, 0, r))
    return pl.pallas_call(
        functools.partial(_dft_kernel, scale=1.0 / math.sqrt(SEQ_LEN * GROUP_DIM)),
        grid=(GROUPS, BATCH, RADIX),
        in_specs=[_resident((DFT_ROWS, DFT_K)), _resident((DFT_ROWS, DFT_K)), in_spec, in_spec],
        out_specs=pl.BlockSpec((None, None, DFT_ROWS, GROUP_DIM), lambda g, b, r: (g, b, 0, r)),
        out_shape=jax.ShapeDtypeStruct((GROUPS, BATCH, DFT_LEN, MIX_WIDTH), BF16),
        compiler_params=_params(("parallel", "parallel", "parallel")),
        name="dft_matmul",
    )(cm, sm, p, q)


def _even_out_kernel(h_ref, ya_ref, xm_ref, xp_ref, xnx_ref, cw_ref, cb_ref, lg_ref, lb_ref,
                     w_ref, g_ref, ho_ref, xo_ref, xpad, xshift, ycat):
    i = pl.program_id(0)
    tiles_per_seq = SEQ_LEN // TM_OUT
    first = (i % tiles_per_seq) == 0
    last = (i % tiles_per_seq) == tiles_per_seq - 1
    xpad[0:HALO_EVEN, :] = jnp.where(first, 0.0, xp_ref[...])
    xpad[HALO_EVEN:HALO_EVEN + TM_OUT, :] = xm_ref[...]
    xpad[HALO_EVEN + TM_OUT:, :] = jnp.where(last, 0.0, xnx_ref[...])
    for g in range(GROUPS):
        ycat[:, g * GROUP_DIM:(g + 1) * GROUP_DIM] = ya_ref[g]

    for g in range(GROUPS):
        lanes = slice(g * GROUP_DIM, (g + 1) * GROUP_DIM)
        out_lanes = slice(MIX_WIDTH + g * GROUP_DIM, MIX_WIDTH + (g + 1) * GROUP_DIM)
        for s in range(8):
            xshift[s] = xpad[s:s + SHIFT_ROWS, lanes]

        def chunk(c, carry, lanes=lanes, out_lanes=out_lanes):
            r0 = pl.multiple_of(c * CONV_RC, 16)
            acc = jnp.zeros((CONV_RC, GROUP_DIM), F32)
            for j in range(CONV_KERNEL):
                oq, os_ = divmod(HALO_EVEN - CONV_PAD + j, 8)
                start = pl.multiple_of(r0 + 8 * oq, 8)
                acc = acc + cw_ref[j:j + 1, lanes] * xshift[os_, pl.ds(start, CONV_RC), :]
            acc = acc + cb_ref[:, lanes]
            mu = jnp.mean(acc, axis=-1, keepdims=True)
            dev = acc - mu
            var = jnp.mean(dev * dev, axis=-1, keepdims=True)
            y = dev * lax.rsqrt(var + EPS) * lg_ref[:, lanes] + lb_ref[:, lanes]
            y = y * jax.nn.sigmoid(y)
            ycat[pl.ds(r0, CONV_RC), out_lanes] = y.astype(BF16)
            return carry

        lax.fori_loop(0, TM_OUT // CONV_RC, chunk, 0)

    hn = h_ref[...] + jnp.dot(ycat[...], w_ref[...], preferred_element_type=F32)
    ho_ref[...] = hn
    xo_ref[...] = _rms_gain(hn, g_ref[...]).astype(xo_ref.dtype)


def _even_out_call(h, ya, glu, cw, cb, lg, lb, w, g):
    hb = TM_OUT // HALO_EVEN
    n_halo = ROWS // HALO_EVEN
    row = lambda i: (i, 0)
    const = lambda i: (0, 0)
    return pl.pallas_call(
        _even_out_kernel,
        grid=(ROWS // TM_OUT,),
        in_specs=[pl.BlockSpec((TM_OUT, D_MODEL), row),
                  pl.BlockSpec((GROUPS, TM_OUT, GROUP_DIM), lambda i: (0, i, 0)),
                  pl.BlockSpec((TM_OUT, MIX_WIDTH), row),
                  pl.BlockSpec((HALO_EVEN, MIX_WIDTH), lambda i: (jnp.maximum(i * hb - 1, 0), 0)),
                  pl.BlockSpec((HALO_EVEN, MIX_WIDTH),
                               lambda i: (jnp.minimum((i + 1) * hb, n_halo - 1), 0)),
                  pl.BlockSpec((CONV_KERNEL, MIX_WIDTH), const),
                  pl.BlockSpec((1, MIX_WIDTH), const),
                  pl.BlockSpec((1, MIX_WIDTH), const),
                  pl.BlockSpec((1, MIX_WIDTH), const),
                  _resident((D_MODEL, D_MODEL)),
                  pl.BlockSpec((1, D_MODEL), const)],
        out_specs=[pl.BlockSpec((TM_OUT, D_MODEL), row), pl.BlockSpec((TM_OUT, D_MODEL), row)],
        out_shape=[jax.ShapeDtypeStruct((ROWS, D_MODEL), F32),
                   jax.ShapeDtypeStruct((ROWS, D_MODEL), BF16)],
        scratch_shapes=[pltpu.VMEM((TM_OUT + 2 * HALO_EVEN, MIX_WIDTH), F32),
                        pltpu.VMEM((8, SHIFT_ROWS, GROUP_DIM), F32),
                        pltpu.VMEM((TM_OUT, D_MODEL), BF16)],
        compiler_params=_params(("parallel",)),
        name="even_out_proj",
    )(h, ya, glu, glu, glu, cw, cb, lg, lb, w, g)


def _odd_in_kernel(xn_ref, w_ref, gb_ref, p_ref):
    proj = jnp.dot(xn_ref[...], w_ref[...], preferred_element_type=F32)
    gb_ref[...] = proj[:, :ODD_CHUNK]
    p_ref[...] = proj[:, ODD_CHUNK:2 * ODD_CHUNK] * proj[:, 2 * ODD_CHUNK:]


def _odd_in_call(xn, w):
    n_chunks = D_MODEL // ODD_CHUNK
    out = jax.ShapeDtypeStruct((ROWS, D_MODEL), F32)
    return pl.pallas_call(
        _odd_in_kernel,
        grid=(ROWS // TM_IN, n_chunks),
        in_specs=[pl.BlockSpec((TM_IN, D_MODEL), lambda i, n: (i, 0)),
                  pl.BlockSpec((None, D_MODEL, 3 * ODD_CHUNK), lambda i, n: (n, 0, 0))],
        out_specs=[pl.BlockSpec((TM_IN, ODD_CHUNK), lambda i, n: (i, n)),
                   pl.BlockSpec((TM_IN, ODD_CHUNK), lambda i, n: (i, n))],
        out_shape=[out, out],
        compiler_params=_params(("parallel", "arbitrary")),
        name="odd_in_proj",
    )(xn, w)


def _odd_out_kernel(h_ref, gb_ref, pm_ref, pp_ref, pn_ref, cw_ref, w_ref, g_ref,
                    ho_ref, xo_ref, xprev, xnext, ybuf):
    i = pl.program_id(0)
    tiles_per_seq = SEQ_LEN // TM_OUT
    first = (i % tiles_per_seq) == 0
    last = (i % tiles_per_seq) == tiles_per_seq - 1
    xprev[0:1, :] = jnp.where(first, 0.0, pp_ref[HALO_ODD - 1:HALO_ODD, :])
    xprev[1:TM_OUT, :] = pm_ref[0:TM_OUT - 1, :]
    xnext[0:TM_OUT - 1, :] = pm_ref[1:TM_OUT, :]
    xnext[TM_OUT - 1:TM_OUT, :] = jnp.where(last, 0.0, pn_ref[0:1, :])

    half = D_MODEL // 2
    for blk in range(2):
        lanes = slice(blk * half, (blk + 1) * half)

        def chunk(c, carry, lanes=lanes):
            r0 = pl.multiple_of(c * 16, 16)
            rows = pl.ds(r0, 16)
            conv = cw_ref[0:1, lanes] * xprev[rows, lanes]
            conv = conv + cw_ref[1:2, lanes] * pm_ref[rows, lanes]
            conv = conv + cw_ref[2:3, lanes] * xnext[rows, lanes]
            ybuf[rows, lanes] = (gb_ref[rows, lanes] * conv).astype(BF16)
            return carry

        lax.fori_loop(0, TM_OUT // 16, chunk, 0)

    hn = h_ref[...] + jnp.dot(ybuf[...], w_ref[...], preferred_element_type=F32)
    ho_ref[...] = hn
    xo_ref[...] = _rms_gain(hn, g_ref[...]).astype(xo_ref.dtype)


def _odd_out_call(h, gb, p, cw, w, g):
    hb = TM_OUT // HALO_ODD
    n_halo = ROWS // HALO_ODD
    row = lambda i: (i, 0)
    const = lambda i: (0, 0)
    return pl.pallas_call(
        _odd_out_kernel,
        grid=(ROWS // TM_OUT,),
        in_specs=[pl.BlockSpec((TM_OUT, D_MODEL), row),
                  pl.BlockSpec((TM_OUT, D_MODEL), row),
                  pl.BlockSpec((TM_OUT, D_MODEL), row),
                  pl.BlockSpec((HALO_ODD, D_MODEL), lambda i: (jnp.maximum(i * hb - 1, 0), 0)),
                  pl.BlockSpec((HALO_ODD, D_MODEL),
                               lambda i: (jnp.minimum((i + 1) * hb, n_halo - 1), 0)),
                  pl.BlockSpec((3, D_MODEL), const),
                  _resident((D_MODEL, D_MODEL)),
                  pl.BlockSpec((1, D_MODEL), const)],
        out_specs=[pl.BlockSpec((TM_OUT, D_MODEL), row), pl.BlockSpec((TM_OUT, D_MODEL), row)],
        out_shape=[jax.ShapeDtypeStruct((ROWS, D_MODEL), F32),
                   jax.ShapeDtypeStruct((ROWS, D_MODEL), BF16)],
        scratch_shapes=[pltpu.VMEM((TM_OUT, D_MODEL), F32),
                        pltpu.VMEM((TM_OUT, D_MODEL), F32),
                        pltpu.VMEM((TM_OUT, D_MODEL), BF16)],
        compiler_params=_params(("parallel",)),
        name="odd_out_proj",
    )(h, gb, p, p, p, cw, w, g)


def _mlp_kernel(h_ref, xn_ref, w1_ref, w2_ref, g_ref, ho_ref, xo_ref):
    f = pl.program_id(1)

    @pl.when(f == 0)
    def _():
        ho_ref[...] = h_ref[...]

    mid = jnp.dot(xn_ref[...], w1_ref[...], preferred_element_type=F32)
    mid = jnp.maximum(mid, 0.0)
    act = (mid * mid).astype(BF16)
    ho_ref[...] += jnp.dot(act, w2_ref[...], preferred_element_type=F32)

    @pl.when(f == pl.num_programs(1) - 1)
    def _():
        xo_ref[...] = _rms_gain(ho_ref[...], g_ref[...]).astype(xo_ref.dtype)


def _mlp_call(h, xn, w1, w2, g, out_dtype):
    row = lambda i, f: (i, 0)
    return pl.pallas_call(
        _mlp_kernel,
        grid=(ROWS // TM_MLP, D_FF // TF_MLP),
        in_specs=[pl.BlockSpec((TM_MLP, D_MODEL), row),
                  pl.BlockSpec((TM_MLP, D_MODEL), row),
                  pl.BlockSpec((D_MODEL, TF_MLP), lambda i, f: (0, f)),
                  pl.BlockSpec((TF_MLP, D_MODEL), lambda i, f: (f, 0)),
                  pl.BlockSpec((1, D_MODEL), lambda i, f: (0, 0))],
        out_specs=[pl.BlockSpec((TM_MLP, D_MODEL), row), pl.BlockSpec((TM_MLP, D_MODEL), row)],
        out_shape=[jax.ShapeDtypeStruct((ROWS, D_MODEL), F32),
                   jax.ShapeDtypeStruct((ROWS, D_MODEL), out_dtype)],
        compiler_params=_params(("parallel", "arbitrary")),
        name="sq_relu_mlp",
    )(h, xn, w1, w2, g)


def _channel_dft_table():
    c = jnp.arange(GROUP_DIM, dtype=jnp.int32)
    ang = ((c[:, None] * c[None, :]) % GROUP_DIM).astype(F32) * (2.0 * math.pi / GROUP_DIM)
    return jnp.concatenate([jnp.cos(ang), jnp.sin(ang)], axis=1).astype(BF16)


def _sequence_dft_tables():
    m = jnp.arange(DFT_ROWS, dtype=jnp.int32)[:, None]
    k = jnp.arange(DFT_K, dtype=jnp.int32)[None, :]
    ang = ((m * k) % DFT_LEN).astype(F32) * (2.0 * math.pi / DFT_LEN)
    live = (m < DFT_LEN) & (k < DFT_LEN)
    cm = jnp.where(live, jnp.cos(ang), 0.0).astype(BF16)
    sm = jnp.where(live, jnp.sin(ang), 0.0).astype(BF16)
    return cm, sm


def _twiddle_table():
    k = jnp.arange(DFT_K, dtype=jnp.int32)
    cols = []
    for r in range(1, RADIX):
        ang = (r * k).astype(F32) * (2.0 * math.pi / SEQ_LEN)
        cols += [jnp.cos(ang), jnp.sin(ang)]
    cols += [jnp.zeros_like(cols[0])] * (8 - len(cols))
    tw = jnp.stack(cols, axis=1)
    return jnp.where((k < DFT_LEN)[:, None], tw, 0.0)


def kernel(x, meta_tokens, norm_mix_g, norm_mlp_g, norm_final_g, ab_w_in, ab_w_out, ab_conv_w,
           ab_conv_b, ab_ln_g, ab_ln_b, c_w_in, c_conv_w, c_w_out, mlp_w1, mlp_w2):
    depth = norm_mix_g.shape[0]
    meta = jnp.broadcast_to(meta_tokens[None].astype(x.dtype), (BATCH, N_META, D_MODEL))
    h = jnp.concatenate([meta, x], axis=1).reshape(ROWS, D_MODEL)

    cs = _channel_dft_table()
    cm, sm = _sequence_dft_tables()
    tw = _twiddle_table()

    xn = _norm_call(h, norm_mix_g[0][None, :])
    for layer in range(depth):
        i = layer // 2
        g_mlp = norm_mlp_g[layer][None, :]
        if layer % 2 == 0:
            w_in = (ab_w_in[i].reshape(D_MODEL, 3, GROUPS, GROUP_DIM).transpose(2, 0, 1, 3)
                    .reshape(GROUPS, D_MODEL, 3 * GROUP_DIM).astype(BF16))
            uc, us, glu = _even_in_call(xn, w_in, cs)
            shape5 = (GROUPS, BATCH, RADIX, DFT_LEN, GROUP_DIM)
            p, q = _butterfly_call(uc.reshape(shape5), us.reshape(shape5), tw)
            ya = _dft_call(cm, sm, p, q).reshape(GROUPS, ROWS, GROUP_DIM)
            h, xn = _even_out_call(h, ya, glu, ab_conv_w[i], ab_conv_b[i][None, :],
                                   ab_ln_g[i][None, :], ab_ln_b[i][None, :],
                                   ab_w_out[i].astype(BF16), g_mlp)
        else:
            n_chunks = D_MODEL // ODD_CHUNK
            w_in = (c_w_in[i].reshape(D_MODEL, 3, n_chunks, ODD_CHUNK).transpose(2, 0, 1, 3)
                    .reshape(n_chunks, D_MODEL, 3 * ODD_CHUNK).astype(BF16))
            gb, p = _odd_in_call(xn, w_in)
            h, xn = _odd_out_call(h, gb, p, c_conv_w[i], c_w_out[i].astype(BF16), g_mlp)
        final = layer == depth - 1
        g_next = norm_final_g if final else norm_mix_g[layer + 1]
        h, xn = _mlp_call(h, xn, mlp_w1[layer].astype(BF16), mlp_w2[layer].astype(BF16),
                          g_next[None, :], F32 if final else BF16)
    return xn.reshape(BATCH, SEQ_LEN, D_MODEL)[:, N_META:, :]
```

```python
import functools
import math

import jax
import jax.numpy as jnp
from jax import lax
from jax.experimental import pallas as pl
from jax.experimental.pallas import tpu as pltpu

F32 = jnp.float32
BF16 = jnp.bfloat16

D_MODEL = 2048
BATCH = 2
SEQ = 8192
N_META = 16
SEQ_LEN = SEQ + N_META
ROWS = BATCH * SEQ_LEN
GROUPS = 4
GROUP_DIM = 256
MIX_WIDTH = GROUPS * GROUP_DIM
CONV_KERNEL = 31
CONV_PAD = (CONV_KERNEL - 1) // 2
D_FF = 4 * D_MODEL
EPS = 1e-6

RADIX = 4
DFT_LEN = SEQ_LEN // RADIX
DFT_ROWS = 2064
DFT_K = 2304
BFLY_ROWS = DFT_K // 3

TM_IN = 912
TM_OUT = 432
TM_MLP = 912
TF_MLP = 1024
NF_MLP = D_FF // TF_MLP
ODD_CHUNK = 512
HALO_EVEN = 16
HALO_ODD = 8
CONV_RC = 48
SHIFT_ROWS = TM_OUT + 2 * HALO_EVEN - 8
VMEM_LIMIT = 58 * 1024 * 1024

assert SEQ_LEN % TM_IN == 0 and SEQ_LEN % TM_OUT == 0 and ROWS % TM_MLP == 0
assert TM_OUT % CONV_RC == 0 and TM_OUT % HALO_EVEN == 0 and CONV_RC % 16 == 0


def _params(sem):
    return pltpu.CompilerParams(dimension_semantics=sem, vmem_limit_bytes=VMEM_LIMIT)


def _rms_gain(x, g):
    ms = jnp.mean(x * x, axis=-1, keepdims=True)
    return x * lax.rsqrt(ms + EPS) * g


def _resident(shape):
    nd = len(shape)
    return pl.BlockSpec(shape, lambda *_: (0,) * nd, pipeline_mode=pl.Buffered(1))


def _resident_layer(shape, layer):
    nd = len(shape)
    return pl.BlockSpec((None,) + shape, lambda *_: (layer,) + (0,) * nd,
                        pipeline_mode=pl.Buffered(1))


def _norm_kernel(h_ref, g_ref, o_ref):
    o_ref[...] = _rms_gain(h_ref[...], g_ref[...]).astype(o_ref.dtype)


def _norm_call(h, g):
    return pl.pallas_call(
        _norm_kernel,
        grid=(ROWS // TM_IN,),
        in_specs=[pl.BlockSpec((TM_IN, D_MODEL), lambda i: (i, 0)),
                  pl.BlockSpec((1, D_MODEL), lambda i: (0, 0))],
        out_specs=pl.BlockSpec((TM_IN, D_MODEL), lambda i: (i, 0)),
        out_shape=jax.ShapeDtypeStruct((ROWS, D_MODEL), BF16),
        compiler_params=_params(("parallel",)),
        name="prologue_norm",
    )(h, g)


def _even_in_kernel(xn_ref, wu_ref, wa_ref, wg_ref, cs_ref, uc_ref, us_ref, glu_ref):
    xn = xn_ref[...]
    u = jnp.dot(xn, wu_ref[...], preferred_element_type=F32).astype(BF16)
    cs = jnp.dot(u, cs_ref[...], preferred_element_type=F32)
    uc_ref[...] = cs[:, :GROUP_DIM].astype(BF16)
    us_ref[...] = cs[:, GROUP_DIM:].astype(BF16)
    a = jnp.dot(xn, wa_ref[...], preferred_element_type=F32)
    gate = jnp.dot(xn, wg_ref[...], preferred_element_type=F32)
    glu_ref[...] = a * jax.nn.sigmoid(gate)


def _even_in_call(xn, w, cs, layer):
    grp = jax.ShapeDtypeStruct((GROUPS, ROWS, GROUP_DIM), BF16)
    wspec = lambda part: pl.BlockSpec((None, D_MODEL, GROUP_DIM),
                                      lambda i, g: (layer, 0, part * GROUPS + g))
    return pl.pallas_call(
        _even_in_kernel,
        grid=(ROWS // TM_IN, GROUPS),
        in_specs=[pl.BlockSpec((TM_IN, D_MODEL), lambda i, g: (i, 0)),
                  wspec(0), wspec(1), wspec(2),
                  pl.BlockSpec((GROUP_DIM, 2 * GROUP_DIM), lambda i, g: (0, 0))],
        out_specs=[pl.BlockSpec((None, TM_IN, GROUP_DIM), lambda i, g: (g, i, 0)),
                   pl.BlockSpec((None, TM_IN, GROUP_DIM), lambda i, g: (g, i, 0)),
                   pl.BlockSpec((TM_IN, GROUP_DIM), lambda i, g: (i, g))],
        out_shape=[grp, grp, jax.ShapeDtypeStruct((ROWS, MIX_WIDTH), F32)],
        compiler_params=_params(("parallel", "arbitrary")),
        name="even_in_proj",
    )(xn, w, w, w, cs)


def _butterfly_kernel(uc_ref, us_ref, tw_ref, p_ref, q_ref):
    kb = pl.program_id(2)

    def chunk(c, carry):
        r0 = pl.multiple_of(c * 16, 16)
        rows = pl.ds(r0, 16)
        a = [uc_ref[k, rows, :].astype(F32) for k in range(RADIX)]
        b = [us_ref[k, rows, :].astype(F32) for k in range(RADIX)]
        tw = tw_ref[rows, :]
        row_id = kb * BFLY_ROWS + r0 + lax.broadcasted_iota(jnp.int32, (16, 1), 0)
        valid = row_id < DFT_LEN
        sa02, da02 = a[0] + a[2], a[0] - a[2]
        sa13, da13 = a[1] + a[3], a[1] - a[3]
        sb02, db02 = b[0] + b[2], b[0] - b[2]
        sb13, db13 = b[1] + b[3], b[1] - b[3]
        xs = [sa02 + sa13, da02 - db13, sa02 - sa13, da02 + db13]
        ys = [-(sb02 + sb13), -(db02 + da13), sb13 - sb02, da13 - db02]
        for r in range(RADIX):
            if r == 0:
                p, q = xs[0], ys[0]
            else:
                cr = tw[:, 2 * r - 2:2 * r - 1]
                sr = tw[:, 2 * r - 1:2 * r]
                p = cr * xs[r] + sr * ys[r]
                q = cr * ys[r] - sr * xs[r]
            lanes = slice(r * GROUP_DIM, (r + 1) * GROUP_DIM)
            p_ref[rows, lanes] = jnp.where(valid, p, 0.0).astype(BF16)
            q_ref[rows, lanes] = jnp.where(valid, q, 0.0).astype(BF16)
        return carry

    lax.fori_loop(0, BFLY_ROWS // 16, chunk, 0)


def _butterfly_call(uc, us, tw):
    in_spec = pl.BlockSpec((None, None, RADIX, BFLY_ROWS, GROUP_DIM),
                           lambda g, b, k: (g, b, 0, k, 0))
    out_spec = pl.BlockSpec((None, None, BFLY_ROWS, MIX_WIDTH), lambda g, b, k: (g, b, k, 0))
    out = jax.ShapeDtypeStruct((GROUPS, BATCH, DFT_K, MIX_WIDTH), BF16)
    return pl.pallas_call(
        _butterfly_kernel,
        grid=(GROUPS, BATCH, DFT_K // BFLY_ROWS),
        in_specs=[in_spec, in_spec, pl.BlockSpec((BFLY_ROWS, 8), lambda g, b, k: (k, 0))],
        out_specs=[out_spec, out_spec],
        out_shape=[out, out],
        compiler_params=_params(("parallel", "parallel", "parallel")),
        name="dft_butterfly",
    )(uc, us, tw)


def _dft_kernel(c_ref, s_ref, p_ref, q_ref, o_ref, *, scale):
    acc = jnp.dot(c_ref[...], p_ref[...], preferred_element_type=F32)
    acc = acc + jnp.dot(s_ref[...], q_ref[...], preferred_element_type=F32)
    o_ref[...] = (acc * scale).astype(o_ref.dtype)


def _dft_call(cm, sm, p, q):
    in_spec = pl.BlockSpec((None, None, DFT_K, GROUP_DIM), lambda g, b, r: (g, b, 0, r))
    return pl.pallas_call(
        functools.partial(_dft_kernel, scale=1.0 / math.sqrt(SEQ_LEN * GROUP_DIM)),
        grid=(GROUPS, BATCH, RADIX),
        in_specs=[_resident((DFT_ROWS, DFT_K)), _resident((DFT_ROWS, DFT_K)), in_spec, in_spec],
        out_specs=pl.BlockSpec((None, None, DFT_ROWS, GROUP_DIM), lambda g, b, r: (g, b, 0, r)),
        out_shape=jax.ShapeDtypeStruct((GROUPS, BATCH, DFT_LEN, MIX_WIDTH), BF16),
        compiler_params=_params(("parallel", "parallel", "parallel")),
        name="dft_matmul",
    )(cm, sm, p, q)


def _even_out_kernel(h_ref, ya_ref, xm_ref, xp_ref, xnx_ref, cw_ref, cb_ref, lg_ref, lb_ref,
                     w_ref, g_ref, ho_ref, xo_ref, xpad, xshift, ycat):
    i = pl.program_id(0)
    tiles_per_seq = SEQ_LEN // TM_OUT
    first = (i % tiles_per_seq) == 0
    last = (i % tiles_per_seq) == tiles_per_seq - 1
    xpad[0:HALO_EVEN, :] = jnp.where(first, 0.0, xp_ref[...])
    xpad[HALO_EVEN:HALO_EVEN + TM_OUT, :] = xm_ref[...]
    xpad[HALO_EVEN + TM_OUT:, :] = jnp.where(last, 0.0, xnx_ref[...])
    for g in range(GROUPS):
        ycat[:, g * GROUP_DIM:(g + 1) * GROUP_DIM] = ya_ref[g]

    for g in range(GROUPS):
        lanes = slice(g * GROUP_DIM, (g + 1) * GROUP_DIM)
        out_lanes = slice(MIX_WIDTH + g * GROUP_DIM, MIX_WIDTH + (g + 1) * GROUP_DIM)
        for s in range(8):
            xshift[s] = xpad[s:s + SHIFT_ROWS, lanes]

        def chunk(c, carry, lanes=lanes, out_lanes=out_lanes):
            r0 = pl.multiple_of(c * CONV_RC, 16)
            acc = jnp.zeros((CONV_RC, GROUP_DIM), F32)
            for j in range(CONV_KERNEL):
                oq, os_ = divmod(HALO_EVEN - CONV_PAD + j, 8)
                start = pl.multiple_of(r0 + 8 * oq, 8)
                acc = acc + cw_ref[j:j + 1, lanes] * xshift[os_, pl.ds(start, CONV_RC), :]
            acc = acc + cb_ref[:, lanes]
            mu = jnp.mean(acc, axis=-1, keepdims=True)
            dev = acc - mu
            var = jnp.mean(dev * dev, axis=-1, keepdims=True)
            y = dev * lax.rsqrt(var + EPS) * lg_ref[:, lanes] + lb_ref[:, lanes]
            y = y * jax.nn.sigmoid(y)
            ycat[pl.ds(r0, CONV_RC), out_lanes] = y.astype(BF16)
            return carry

        lax.fori_loop(0, TM_OUT // CONV_RC, chunk, 0)

    hn = h_ref[...] + jnp.dot(ycat[...], w_ref[...], preferred_element_type=F32)
    ho_ref[...] = hn
    xo_ref[...] = _rms_gain(hn, g_ref[...]).astype(xo_ref.dtype)


def _even_out_call(h, ya, glu, cw, cb, lg, lb, w, g, layer):
    hb = TM_OUT // HALO_EVEN
    n_halo = ROWS // HALO_EVEN
    row = lambda i: (i, 0)
    const = lambda i: (0, 0)
    return pl.pallas_call(
        _even_out_kernel,
        grid=(ROWS // TM_OUT,),
        in_specs=[pl.BlockSpec((TM_OUT, D_MODEL), row),
                  pl.BlockSpec((GROUPS, TM_OUT, GROUP_DIM), lambda i: (0, i, 0)),
                  pl.BlockSpec((TM_OUT, MIX_WIDTH), row),
                  pl.BlockSpec((HALO_EVEN, MIX_WIDTH), lambda i: (jnp.maximum(i * hb - 1, 0), 0)),
                  pl.BlockSpec((HALO_EVEN, MIX_WIDTH),
                               lambda i: (jnp.minimum((i + 1) * hb, n_halo - 1), 0)),
                  pl.BlockSpec((CONV_KERNEL, MIX_WIDTH), const),
                  pl.BlockSpec((1, MIX_WIDTH), const),
                  pl.BlockSpec((1, MIX_WIDTH), const),
                  pl.BlockSpec((1, MIX_WIDTH), const),
                  _resident_layer((D_MODEL, D_MODEL), layer),
                  pl.BlockSpec((1, D_MODEL), const)],
        out_specs=[pl.BlockSpec((TM_OUT, D_MODEL), row), pl.BlockSpec((TM_OUT, D_MODEL), row)],
        out_shape=[jax.ShapeDtypeStruct((ROWS, D_MODEL), F32),
                   jax.ShapeDtypeStruct((ROWS, D_MODEL), BF16)],
        scratch_shapes=[pltpu.VMEM((TM_OUT + 2 * HALO_EVEN, MIX_WIDTH), F32),
                        pltpu.VMEM((8, SHIFT_ROWS, GROUP_DIM), F32),
                        pltpu.VMEM((TM_OUT, D_MODEL), BF16)],
        compiler_params=_params(("parallel",)),
        name="even_out_proj",
    )(h, ya, glu, glu, glu, cw, cb, lg, lb, w, g)


def _odd_in_kernel(xn_ref, wb_ref, wc_ref, wh_ref, gb_ref, p_ref):
    xn = xn_ref[...]
    gb_ref[...] = jnp.dot(xn, wb_ref[...], preferred_element_type=F32)
    gc = jnp.dot(xn, wc_ref[...], preferred_element_type=F32)
    hin = jnp.dot(xn, wh_ref[...], preferred_element_type=F32)
    p_ref[...] = gc * hin


def _odd_in_call(xn, w, layer):
    n_chunks = D_MODEL // ODD_CHUNK
    out = jax.ShapeDtypeStruct((ROWS, D_MODEL), F32)
    wspec = lambda part: pl.BlockSpec((None, D_MODEL, ODD_CHUNK),
                                      lambda i, n: (layer, 0, part * n_chunks + n))
    return pl.pallas_call(
        _odd_in_kernel,
        grid=(ROWS // TM_IN, n_chunks),
        in_specs=[pl.BlockSpec((TM_IN, D_MODEL), lambda i, n: (i, 0)),
                  wspec(0), wspec(1), wspec(2)],
        out_specs=[pl.BlockSpec((TM_IN, ODD_CHUNK), lambda i, n: (i, n)),
                   pl.BlockSpec((TM_IN, ODD_CHUNK), lambda i, n: (i, n))],
        out_shape=[out, out],
        compiler_params=_params(("parallel", "arbitrary")),
        name="odd_in_proj",
    )(xn, w, w, w)


def _odd_out_kernel(h_ref, gb_ref, pm_ref, pp_ref, pn_ref, cw_ref, w_ref, g_ref,
                    ho_ref, xo_ref, xprev, xnext, ybuf):
    i = pl.program_id(0)
    tiles_per_seq = SEQ_LEN // TM_OUT
    first = (i % tiles_per_seq) == 0
    last = (i % tiles_per_seq) == tiles_per_seq - 1
    xprev[0:1, :] = jnp.where(first, 0.0, pp_ref[HALO_ODD - 1:HALO_ODD, :])
    xprev[1:TM_OUT, :] = pm_ref[0:TM_OUT - 1, :]
    xnext[0:TM_OUT - 1, :] = pm_ref[1:TM_OUT, :]
    xnext[TM_OUT - 1:TM_OUT, :] = jnp.where(last, 0.0, pn_ref[0:1, :])

    half = D_MODEL // 2
    for blk in range(2):
        lanes = slice(blk * half, (blk + 1) * half)

        def chunk(c, carry, lanes=lanes):
            r0 = pl.multiple_of(c * 16, 16)
            rows = pl.ds(r0, 16)
            conv = cw_ref[0:1, lanes] * xprev[rows, lanes]
            conv = conv + cw_ref[1:2, lanes] * pm_ref[rows, lanes]
            conv = conv + cw_ref[2:3, lanes] * xnext[rows, lanes]
            ybuf[rows, lanes] = (gb_ref[rows, lanes] * conv).astype(BF16)
            return carry

        lax.fori_loop(0, TM_OUT // 16, chunk, 0)

    hn = h_ref[...] + jnp.dot(ybuf[...], w_ref[...], preferred_element_type=F32)
    ho_ref[...] = hn
    xo_ref[...] = _rms_gain(hn, g_ref[...]).astype(xo_ref.dtype)


def _odd_out_call(h, gb, p, cw, w, g, layer):
    hb = TM_OUT // HALO_ODD
    n_halo = ROWS // HALO_ODD
    row = lambda i: (i, 0)
    const = lambda i: (0, 0)
    return pl.pallas_call(
        _odd_out_kernel,
        grid=(ROWS // TM_OUT,),
        in_specs=[pl.BlockSpec((TM_OUT, D_MODEL), row),
                  pl.BlockSpec((TM_OUT, D_MODEL), row),
                  pl.BlockSpec((TM_OUT, D_MODEL), row),
                  pl.BlockSpec((HALO_ODD, D_MODEL), lambda i: (jnp.maximum(i * hb - 1, 0), 0)),
                  pl.BlockSpec((HALO_ODD, D_MODEL),
                               lambda i: (jnp.minimum((i + 1) * hb, n_halo - 1), 0)),
                  pl.BlockSpec((3, D_MODEL), const),
                  _resident_layer((D_MODEL, D_MODEL), layer),
                  pl.BlockSpec((1, D_MODEL), const)],
        out_specs=[pl.BlockSpec((TM_OUT, D_MODEL), row), pl.BlockSpec((TM_OUT, D_MODEL), row)],
        out_shape=[jax.ShapeDtypeStruct((ROWS, D_MODEL), F32),
                   jax.ShapeDtypeStruct((ROWS, D_MODEL), BF16)],
        scratch_shapes=[pltpu.VMEM((TM_OUT, D_MODEL), F32),
                        pltpu.VMEM((TM_OUT, D_MODEL), F32),
                        pltpu.VMEM((TM_OUT, D_MODEL), BF16)],
        compiler_params=_params(("parallel",)),
        name="odd_out_proj",
    )(h, gb, p, p, p, cw, w, g)


def _mlp_kernel(h_hbm, xn_ref, w1_hbm, w2_hbm, g_ref, ho_ref, xo_ref, w1buf, w2buf, wsem, hsem,
                *, layer):
    i = pl.program_id(0)
    n_tiles = pl.num_programs(0)

    def weight_copies(f, slot):
        cols = pl.ds(pl.multiple_of(f * TF_MLP, TF_MLP), TF_MLP)
        return (pltpu.make_async_copy(w1_hbm.at[layer, :, cols], w1buf.at[slot], wsem.at[0, slot]),
                pltpu.make_async_copy(w2_hbm.at[layer, cols, :], w2buf.at[slot], wsem.at[1, slot]))

    def start_weights(f, slot):
        for c in weight_copies(f, slot):
            c.start()

    rows = pl.ds(pl.multiple_of(i * TM_MLP, 16), TM_MLP)
    h_copy = pltpu.make_async_copy(h_hbm.at[rows, :], ho_ref, hsem.at[0])
    h_copy.start()

    @pl.when(i == 0)
    def _():
        start_weights(0, 0)

    def step(f, carry):
        slot = lax.rem(f, 2)

        @pl.when(f + 1 < NF_MLP)
        def _():
            start_weights(f + 1, 1 - slot)

        @pl.when(jnp.logical_and(f + 1 == NF_MLP, i + 1 < n_tiles))
        def _():
            start_weights(0, 1 - slot)

        for c in weight_copies(f, slot):
            c.wait()
        mid = jnp.dot(xn_ref[...], w1buf[slot], preferred_element_type=F32)
        mid = jnp.maximum(mid, 0.0)
        act = (mid * mid).astype(BF16)

        @pl.when(f == 0)
        def _():
            h_copy.wait()

        ho_ref[...] += jnp.dot(act, w2buf[slot], preferred_element_type=F32)
        return carry

    lax.fori_loop(0, NF_MLP, step, 0)
    xo_ref[...] = _rms_gain(ho_ref[...], g_ref[...]).astype(xo_ref.dtype)


def _mlp_call(h, xn, w1, w2, g, layer, out_dtype):
    assert NF_MLP % 2 == 0
    row = lambda i: (i, 0)
    return pl.pallas_call(
        functools.partial(_mlp_kernel, layer=layer),
        grid=(ROWS // TM_MLP,),
        in_specs=[pl.BlockSpec(memory_space=pl.ANY),
                  pl.BlockSpec((TM_MLP, D_MODEL), row),
                  pl.BlockSpec(memory_space=pl.ANY),
                  pl.BlockSpec(memory_space=pl.ANY),
                  pl.BlockSpec((1, D_MODEL), lambda i: (0, 0))],
        out_specs=[pl.BlockSpec((TM_MLP, D_MODEL), row), pl.BlockSpec((TM_MLP, D_MODEL), row)],
        out_shape=[jax.ShapeDtypeStruct((ROWS, D_MODEL), F32),
                   jax.ShapeDtypeStruct((ROWS, D_MODEL), out_dtype)],
        scratch_shapes=[pltpu.VMEM((2, D_MODEL, TF_MLP), BF16),
                        pltpu.VMEM((2, TF_MLP, D_MODEL), BF16),
                        pltpu.SemaphoreType.DMA((2, 2)),
                        pltpu.SemaphoreType.DMA((1,))],
        compiler_params=_params(("arbitrary",)),
        name="sq_relu_mlp",
    )(h, xn, w1, w2, g)


def _channel_dft_table():
    c = jnp.arange(GROUP_DIM, dtype=jnp.int32)
    ang = ((c[:, None] * c[None, :]) % GROUP_DIM).astype(F32) * (2.0 * math.pi / GROUP_DIM)
    return jnp.concatenate([jnp.cos(ang), jnp.sin(ang)], axis=1).astype(BF16)


def _sequence_dft_tables():
    m = jnp.arange(DFT_ROWS, dtype=jnp.int32)[:, None]
    k = jnp.arange(DFT_K, dtype=jnp.int32)[None, :]
    ang = ((m * k) % DFT_LEN).astype(F32) * (2.0 * math.pi / DFT_LEN)
    live = (m < DFT_LEN) & (k < DFT_LEN)
    cm = jnp.where(live, jnp.cos(ang), 0.0).astype(BF16)
    sm = jnp.where(live, jnp.sin(ang), 0.0).astype(BF16)
    return cm, sm


def _twiddle_table():
    k = jnp.arange(DFT_K, dtype=jnp.int32)
    cols = []
    for r in range(1, RADIX):
        ang = (r * k).astype(F32) * (2.0 * math.pi / SEQ_LEN)
        cols += [jnp.cos(ang), jnp.sin(ang)]
    cols += [jnp.zeros_like(cols[0])] * (8 - len(cols))
    tw = jnp.stack(cols, axis=1)
    return jnp.where((k < DFT_LEN)[:, None], tw, 0.0)


def kernel(x, meta_tokens, norm_mix_g, norm_mlp_g, norm_final_g, ab_w_in, ab_w_out, ab_conv_w,
           ab_conv_b, ab_ln_g, ab_ln_b, c_w_in, c_conv_w, c_w_out, mlp_w1, mlp_w2):
    depth = norm_mix_g.shape[0]
    meta = jnp.broadcast_to(meta_tokens[None].astype(x.dtype), (BATCH, N_META, D_MODEL))
    h = jnp.concatenate([meta, x], axis=1).reshape(ROWS, D_MODEL)

    cs = _channel_dft_table()
    cm, sm = _sequence_dft_tables()
    tw = _twiddle_table()

    ab_w_in, ab_w_out, c_w_in, c_w_out, mlp_w1, mlp_w2 = (
        w.astype(BF16) for w in (ab_w_in, ab_w_out, c_w_in, c_w_out, mlp_w1, mlp_w2))

    xn = _norm_call(h, norm_mix_g[0][None, :])
    for layer in range(depth):
        i = layer // 2
        g_mlp = norm_mlp_g[layer][None, :]
        if layer % 2 == 0:
            uc, us, glu = _even_in_call(xn, ab_w_in, cs, i)
            shape5 = (GROUPS, BATCH, RADIX, DFT_LEN, GROUP_DIM)
            p, q = _butterfly_call(uc.reshape(shape5), us.reshape(shape5), tw)
            ya = _dft_call(cm, sm, p, q).reshape(GROUPS, ROWS, GROUP_DIM)
            h, xn = _even_out_call(h, ya, glu, ab_conv_w[i], ab_conv_b[i][None, :],
                                   ab_ln_g[i][None, :], ab_ln_b[i][None, :], ab_w_out, g_mlp, i)
        else:
            gb, p = _odd_in_call(xn, c_w_in, i)
            h, xn = _odd_out_call(h, gb, p, c_conv_w[i], c_w_out, g_mlp, i)
        final = layer == depth - 1
        g_next = norm_final_g if final else norm_mix_g[layer + 1]
        h, xn = _mlp_call(h, xn, mlp_w1, mlp_w2, g_next[None, :], layer, F32 if final else BF16)
    return xn.reshape(BATCH, SEQ_LEN, D_MODEL)[:, N_META:, :]
```

```python
import functools
import math

import jax
import jax.numpy as jnp
from jax import lax
from jax.experimental import pallas as pl
from jax.experimental.pallas import tpu as pltpu

F32 = jnp.float32
BF16 = jnp.bfloat16

D_MODEL = 2048
BATCH = 2
SEQ = 8192
N_META = 16
SEQ_LEN = SEQ + N_META
ROWS = BATCH * SEQ_LEN
GROUPS = 4
GROUP_DIM = 256
MIX_WIDTH = GROUPS * GROUP_DIM
CONV_KERNEL = 31
CONV_PAD = (CONV_KERNEL - 1) // 2
D_FF = 4 * D_MODEL
EPS = 1e-6

RADIX = 3
DFT_LEN = SEQ_LEN // RADIX
DFT_K = 2816
BFLY_ROWS = DFT_K // 4
DFT_MB = DFT_LEN // 3
TABLE_BLOCK = 48

TM_IN = 912
TM_OUT = 432
TM_MLP = 912
TF_MLP = 1024
NF_MLP = D_FF // TF_MLP
ODD_CHUNK = 512
HALO = 16
SUBLANES = 8
CONV_RC = 48
SHIFT_ROWS = TM_OUT + 2 * HALO - SUBLANES
TILES_PER_SEQ = SEQ_LEN // TM_OUT
DFT_ROWS_PER_TILE = TM_OUT // RADIX
VMEM_LIMIT = 58 * 1024 * 1024

assert SEQ_LEN % TM_IN == 0 and SEQ_LEN % TM_OUT == 0 and ROWS % TM_MLP == 0
assert TM_OUT % CONV_RC == 0 and TM_OUT % HALO == 0 and CONV_RC % 16 == 0
assert DFT_LEN % 16 == 0 and DFT_ROWS_PER_TILE % 16 == 0 and DFT_LEN % TABLE_BLOCK == 0
assert DFT_K % BFLY_ROWS == 0 and BFLY_ROWS % 16 == 0 and NF_MLP % 2 == 0


def _params(sem):
    return pltpu.CompilerParams(dimension_semantics=sem, vmem_limit_bytes=VMEM_LIMIT)


def _rms_gain(x, g):
    ms = jnp.mean(x * x, axis=-1, keepdims=True)
    return x * lax.rsqrt(ms + EPS) * g


def _resident(shape):
    nd = len(shape)
    return pl.BlockSpec(shape, lambda *_: (0,) * nd, pipeline_mode=pl.Buffered(1))


def _resident_layer(shape, layer):
    nd = len(shape)
    return pl.BlockSpec((None,) + shape, lambda *_: (layer,) + (0,) * nd,
                        pipeline_mode=pl.Buffered(1))


def _norm_kernel(h_ref, g_ref, o_ref):
    o_ref[...] = _rms_gain(h_ref[...], g_ref[...]).astype(o_ref.dtype)


def _norm_call(h, g):
    return pl.pallas_call(
        _norm_kernel,
        grid=(ROWS // TM_IN,),
        in_specs=[pl.BlockSpec((TM_IN, D_MODEL), lambda i: (i, 0)),
                  pl.BlockSpec((1, D_MODEL), lambda i: (0, 0))],
        out_specs=pl.BlockSpec((TM_IN, D_MODEL), lambda i: (i, 0)),
        out_shape=jax.ShapeDtypeStruct((ROWS, D_MODEL), BF16),
        compiler_params=_params(("parallel",)),
        name="prologue_norm",
    )(h, g)


def _even_in_kernel(xn_ref, wu_ref, wa_ref, wg_ref, cs_ref, uc_ref, us_ref, glu_ref, wcat):
    wcat[:, 0:GROUP_DIM] = wu_ref[...]
    wcat[:, GROUP_DIM:2 * GROUP_DIM] = wa_ref[...]
    wcat[:, 2 * GROUP_DIM:] = wg_ref[...]
    proj = jnp.dot(xn_ref[...], wcat[...], preferred_element_type=F32)
    u = proj[:, :GROUP_DIM].astype(BF16)
    cs = jnp.dot(u, cs_ref[...], preferred_element_type=F32)
    uc_ref[...] = cs[:, :GROUP_DIM].astype(BF16)
    us_ref[...] = cs[:, GROUP_DIM:].astype(BF16)
    a = proj[:, GROUP_DIM:2 * GROUP_DIM]
    gate = proj[:, 2 * GROUP_DIM:]
    glu_ref[...] = a * jax.nn.sigmoid(gate)


def _even_in_call(xn, w, cs, layer):
    grp = jax.ShapeDtypeStruct((GROUPS, ROWS, GROUP_DIM), BF16)
    wspec = lambda part: pl.BlockSpec((None, D_MODEL, GROUP_DIM),
                                      lambda i, g: (layer, 0, part * GROUPS + g))
    return pl.pallas_call(
        _even_in_kernel,
        grid=(ROWS // TM_IN, GROUPS),
        in_specs=[pl.BlockSpec((TM_IN, D_MODEL), lambda i, g: (i, 0)),
                  wspec(0), wspec(1), wspec(2),
                  pl.BlockSpec((GROUP_DIM, 2 * GROUP_DIM), lambda i, g: (0, 0))],
        out_specs=[pl.BlockSpec((None, TM_IN, GROUP_DIM), lambda i, g: (g, i, 0)),
                   pl.BlockSpec((None, TM_IN, GROUP_DIM), lambda i, g: (g, i, 0)),
                   pl.BlockSpec((TM_IN, GROUP_DIM), lambda i, g: (i, g))],
        out_shape=[grp, grp, jax.ShapeDtypeStruct((ROWS, MIX_WIDTH), F32)],
        scratch_shapes=[pltpu.VMEM((D_MODEL, 3 * GROUP_DIM), BF16)],
        compiler_params=_params(("parallel", "arbitrary")),
        name="even_in_proj",
    )(xn, w, w, w, cs)


def _butterfly_kernel(uc_ref, us_ref, tw_ref, p_ref, q_ref):
    kb = pl.program_id(2)
    half_sqrt3 = 0.5 * math.sqrt(3.0)

    def chunk(c, carry):
        r0 = pl.multiple_of(c * 16, 16)
        rows = pl.ds(r0, 16)
        a = [uc_ref[k, rows, :].astype(F32) for k in range(RADIX)]
        b = [us_ref[k, rows, :].astype(F32) for k in range(RADIX)]
        tw = tw_ref[rows, :]
        row_id = kb * BFLY_ROWS + r0 + lax.broadcasted_iota(jnp.int32, (16, 1), 0)
        valid = row_id < DFT_LEN
        sa, da = a[1] + a[2], a[1] - a[2]
        sb, db = b[1] + b[2], b[1] - b[2]
        xm = a[0] - 0.5 * sa
        ym = 0.5 * sb - b[0]
        xs = [a[0] + sa, xm - half_sqrt3 * db, xm + half_sqrt3 * db]
        ys = [-(b[0] + sb), ym - half_sqrt3 * da, ym + half_sqrt3 * da]
        for r in range(RADIX):
            if r == 0:
                p, q = xs[0], ys[0]
            else:
                cr = tw[:, 2 * r - 2:2 * r - 1]
                sr = tw[:, 2 * r - 1:2 * r]
                p = cr * xs[r] + sr * ys[r]
                q = cr * ys[r] - sr * xs[r]
            lanes = slice(r * GROUP_DIM, (r + 1) * GROUP_DIM)
            p_ref[rows, lanes] = jnp.where(valid, p, 0.0).astype(BF16)
            q_ref[rows, lanes] = jnp.where(valid, q, 0.0).astype(BF16)
        return carry

    lax.fori_loop(0, BFLY_ROWS // 16, chunk, 0)


def _butterfly_call(uc, us, tw):
    in_spec = pl.BlockSpec((None, None, RADIX, BFLY_ROWS, GROUP_DIM),
                           lambda g, b, k: (g, b, 0, k, 0))
    out_spec = pl.BlockSpec((None, None, BFLY_ROWS, RADIX * GROUP_DIM),
                            lambda g, b, k: (g, b, k, 0))
    out = jax.ShapeDtypeStruct((GROUPS, BATCH, DFT_K, RADIX * GROUP_DIM), BF16)
    return pl.pallas_call(
        _butterfly_kernel,
        grid=(GROUPS, BATCH, DFT_K // BFLY_ROWS),
        in_specs=[in_spec, in_spec, pl.BlockSpec((BFLY_ROWS, 8), lambda g, b, k: (k, 0))],
        out_specs=[out_spec, out_spec],
        out_shape=[out, out],
        compiler_params=_params(("parallel", "parallel", "parallel")),
        name="dft_butterfly",
    )(uc, us, tw)


def _dft_kernel(c_ref, s_ref, p_ref, q_ref, o_ref, *, scale):
    acc = jnp.dot(c_ref[...], p_ref[...], preferred_element_type=F32)
    acc = acc + jnp.dot(s_ref[...], q_ref[...], preferred_element_type=F32)
    o_ref[...] = (acc * scale).astype(o_ref.dtype)


def _dft_call(cm, sm, p, q):
    mat_spec = pl.BlockSpec((DFT_MB, DFT_K), lambda m, g, b, r: (m, 0))
    in_spec = pl.BlockSpec((None, None, DFT_K, GROUP_DIM), lambda m, g, b, r: (g, b, 0, r))
    return pl.pallas_call(
        functools.partial(_dft_kernel, scale=1.0 / math.sqrt(SEQ_LEN * GROUP_DIM)),
        grid=(DFT_LEN // DFT_MB, GROUPS, BATCH, RADIX),
        in_specs=[mat_spec, mat_spec, in_spec, in_spec],
        out_specs=pl.BlockSpec((None, None, None, DFT_MB, GROUP_DIM),
                               lambda m, g, b, r: (g, b, r, m, 0)),
        out_shape=jax.ShapeDtypeStruct((GROUPS, BATCH, RADIX, DFT_LEN, GROUP_DIM), BF16),
        compiler_params=_params(("parallel", "parallel", "parallel", "parallel")),
        name="dft_matmul",
    )(cm, sm, p, q)


def _even_out_kernel(h_ref, ya_ref, xm_ref, xp_ref, xnx_ref, cw_ref, cb_ref, lg_ref, lb_ref,
                     perm_ref, w_ref, g_ref, ho_ref, xo_ref, xpad, xshift, yperm, ycat):
    i = pl.program_id(0)
    first = (i % TILES_PER_SEQ) == 0
    last = (i % TILES_PER_SEQ) == TILES_PER_SEQ - 1
    xpad[0:HALO, :] = jnp.where(first, 0.0, xp_ref[...])
    xpad[HALO:HALO + TM_OUT, :] = xm_ref[...]
    xpad[HALO + TM_OUT:, :] = jnp.where(last, 0.0, xnx_ref[...])

    for g in range(GROUPS):
        for r in range(RADIX):
            yperm[r * DFT_ROWS_PER_TILE:(r + 1) * DFT_ROWS_PER_TILE,
                  g * GROUP_DIM:(g + 1) * GROUP_DIM] = ya_ref[g, r]
    ycat[:, :MIX_WIDTH] = jnp.dot(perm_ref[...], yperm[...],
                                  preferred_element_type=F32).astype(BF16)
    ho_ref[...] = h_ref[...] + jnp.dot(ycat[:, :MIX_WIDTH], w_ref[:MIX_WIDTH, :],
                                       preferred_element_type=F32)

    row_vregs = CONV_RC // SUBLANES
    for g in range(GROUPS):
        lanes = slice(g * GROUP_DIM, (g + 1) * GROUP_DIM)
        out_lanes = slice(MIX_WIDTH + g * GROUP_DIM, MIX_WIDTH + (g + 1) * GROUP_DIM)
        for s in range(SUBLANES):
            xshift[g, s] = xpad[s:s + SHIFT_ROWS, lanes]
        for c in range(TM_OUT // CONV_RC):
            r0 = c * CONV_RC
            acc = [jnp.zeros((SUBLANES, GROUP_DIM), F32) for _ in range(row_vregs)]
            for j in range(CONV_KERNEL):
                oq, os_ = divmod(HALO - CONV_PAD + j, SUBLANES)
                w = cw_ref[j, :, lanes]
                for r in range(row_vregs):
                    start = r0 + SUBLANES * (oq + r)
                    acc[r] = acc[r] + w * xshift[g, os_, start:start + SUBLANES, :]
            conv = jnp.concatenate(acc, axis=0) + cb_ref[:, lanes]
            mu = jnp.mean(conv, axis=-1, keepdims=True)
            dev = conv - mu
            var = jnp.mean(dev * dev, axis=-1, keepdims=True)
            y = dev * lax.rsqrt(var + EPS) * lg_ref[:, lanes] + lb_ref[:, lanes]
            y = y * jax.nn.sigmoid(y)
            ycat[r0:r0 + CONV_RC, out_lanes] = y.astype(BF16)
        ho_ref[...] += jnp.dot(ycat[:, out_lanes], w_ref[out_lanes, :],
                               preferred_element_type=F32)

    xo_ref[...] = _rms_gain(ho_ref[...], g_ref[...]).astype(xo_ref.dtype)


def _even_out_call(h, ya, glu, cw, cb, lg, lb, perm, w, g, layer):
    hb = TM_OUT // HALO
    n_halo = ROWS // HALO
    row = lambda i: (i, 0)
    const = lambda i: (0, 0)
    return pl.pallas_call(
        _even_out_kernel,
        grid=(ROWS // TM_OUT,),
        in_specs=[pl.BlockSpec((TM_OUT, D_MODEL), row),
                  pl.BlockSpec((GROUPS, None, RADIX, DFT_ROWS_PER_TILE, GROUP_DIM),
                               lambda i: (0, i // TILES_PER_SEQ, 0, i % TILES_PER_SEQ, 0)),
                  pl.BlockSpec((TM_OUT, MIX_WIDTH), row),
                  pl.BlockSpec((HALO, MIX_WIDTH), lambda i: (jnp.maximum(i * hb - 1, 0), 0)),
                  pl.BlockSpec((HALO, MIX_WIDTH),
                               lambda i: (jnp.minimum((i + 1) * hb, n_halo - 1), 0)),
                  pl.BlockSpec((CONV_KERNEL, SUBLANES, MIX_WIDTH), lambda i: (0, 0, 0)),
                  pl.BlockSpec((1, MIX_WIDTH), const),
                  pl.BlockSpec((1, MIX_WIDTH), const),
                  pl.BlockSpec((1, MIX_WIDTH), const),
                  pl.BlockSpec((TM_OUT, TM_OUT), const),
                  _resident_layer((D_MODEL, D_MODEL), layer),
                  pl.BlockSpec((1, D_MODEL), const)],
        out_specs=[pl.BlockSpec((TM_OUT, D_MODEL), row), pl.BlockSpec((TM_OUT, D_MODEL), row)],
        out_shape=[jax.ShapeDtypeStruct((ROWS, D_MODEL), F32),
                   jax.ShapeDtypeStruct((ROWS, D_MODEL), BF16)],
        scratch_shapes=[pltpu.VMEM((TM_OUT + 2 * HALO, MIX_WIDTH), F32),
                        pltpu.VMEM((GROUPS, SUBLANES, SHIFT_ROWS, GROUP_DIM), F32),
                        pltpu.VMEM((TM_OUT, MIX_WIDTH), BF16),
                        pltpu.VMEM((TM_OUT, D_MODEL), BF16)],
        compiler_params=_params(("parallel",)),
        name="even_out_proj",
    )(h, ya, glu, glu, glu, cw, cb, lg, lb, perm, w, g)


def _odd_in_kernel(xn_ref, wb_ref, wc_ref, wh_ref, gb_ref, p_ref):
    xn = xn_ref[...]
    gb_ref[...] = jnp.dot(xn, wb_ref[...], preferred_element_type=F32).astype(gb_ref.dtype)
    gc = jnp.dot(xn, wc_ref[...], preferred_element_type=F32)
    hin = jnp.dot(xn, wh_ref[...], preferred_element_type=F32)
    p_ref[...] = (gc * hin).astype(p_ref.dtype)


def _odd_in_call(xn, w, layer):
    n_chunks = D_MODEL // ODD_CHUNK
    out = jax.ShapeDtypeStruct((ROWS, D_MODEL), BF16)
    wspec = lambda part: pl.BlockSpec((None, D_MODEL, ODD_CHUNK),
                                      lambda i, n: (layer, 0, part * n_chunks + n))
    return pl.pallas_call(
        _odd_in_kernel,
        grid=(ROWS // TM_IN, n_chunks),
        in_specs=[pl.BlockSpec((TM_IN, D_MODEL), lambda i, n: (i, 0)),
                  wspec(0), wspec(1), wspec(2)],
        out_specs=[pl.BlockSpec((TM_IN, ODD_CHUNK), lambda i, n: (i, n)),
                   pl.BlockSpec((TM_IN, ODD_CHUNK), lambda i, n: (i, n))],
        out_shape=[out, out],
        compiler_params=_params(("parallel", "arbitrary")),
        name="odd_in_proj",
    )(xn, w, w, w)


def _odd_out_kernel(h_ref, gb_ref, pm_ref, pp_ref, pn_ref, cw_ref, w_ref, g_ref,
                    ho_ref, xo_ref, xmid, xprev, xnext, ybuf):
    i = pl.program_id(0)
    first = (i % TILES_PER_SEQ) == 0
    last = (i % TILES_PER_SEQ) == TILES_PER_SEQ - 1
    xmid[...] = pm_ref[...].astype(F32)
    xprev[0:1, :] = jnp.where(first, 0.0, pp_ref[HALO - 1:HALO, :].astype(F32))
    xprev[1:TM_OUT, :] = xmid[0:TM_OUT - 1, :]
    xnext[0:TM_OUT - 1, :] = xmid[1:TM_OUT, :]
    xnext[TM_OUT - 1:TM_OUT, :] = jnp.where(last, 0.0, pn_ref[0:1, :].astype(F32))

    half = D_MODEL // 2
    for blk in range(2):
        lanes = slice(blk * half, (blk + 1) * half)

        def chunk(c, carry, lanes=lanes):
            r0 = pl.multiple_of(c * 16, 16)
            rows = pl.ds(r0, 16)
            conv = cw_ref[0:1, lanes] * xprev[rows, lanes]
            conv = conv + cw_ref[1:2, lanes] * xmid[rows, lanes]
            conv = conv + cw_ref[2:3, lanes] * xnext[rows, lanes]
            ybuf[rows, lanes] = (gb_ref[rows, lanes].astype(F32) * conv).astype(BF16)
            return carry

        lax.fori_loop(0, TM_OUT // 16, chunk, 0)

    hn = h_ref[...] + jnp.dot(ybuf[...], w_ref[...], preferred_element_type=F32)
    ho_ref[...] = hn
    xo_ref[...] = _rms_gain(hn, g_ref[...]).astype(xo_ref.dtype)


def _odd_out_call(h, gb, p, cw, w, g, layer):
    hb = TM_OUT // HALO
    n_halo = ROWS // HALO
    row = lambda i: (i, 0)
    const = lambda i: (0, 0)
    return pl.pallas_call(
        _odd_out_kernel,
        grid=(ROWS // TM_OUT,),
        in_specs=[pl.BlockSpec((TM_OUT, D_MODEL), row),
                  pl.BlockSpec((TM_OUT, D_MODEL), row),
                  pl.BlockSpec((TM_OUT, D_MODEL), row),
                  pl.BlockSpec((HALO, D_MODEL), lambda i: (jnp.maximum(i * hb - 1, 0), 0)),
                  pl.BlockSpec((HALO, D_MODEL),
                               lambda i: (jnp.minimum((i + 1) * hb, n_halo - 1), 0)),
                  pl.BlockSpec((3, D_MODEL), const),
                  _resident_layer((D_MODEL, D_MODEL), layer),
                  pl.BlockSpec((1, D_MODEL), const)],
        out_specs=[pl.BlockSpec((TM_OUT, D_MODEL), row), pl.BlockSpec((TM_OUT, D_MODEL), row)],
        out_shape=[jax.ShapeDtypeStruct((ROWS, D_MODEL), F32),
                   jax.ShapeDtypeStruct((ROWS, D_MODEL), BF16)],
        scratch_shapes=[pltpu.VMEM((TM_OUT, D_MODEL), F32),
                        pltpu.VMEM((TM_OUT, D_MODEL), F32),
                        pltpu.VMEM((TM_OUT, D_MODEL), F32),
                        pltpu.VMEM((TM_OUT, D_MODEL), BF16)],
        compiler_params=_params(("parallel",)),
        name="odd_out_proj",
    )(h, gb, p, p, p, cw, w, g)


def _mlp_accumulate(h_hbm, xn_ref, w1_hbm, w2_hbm, ho_ref, w1buf, w2buf, wsem, hsem, layer):
    i = pl.program_id(0)
    n_tiles = pl.num_programs(0)

    def weight_copies(f, slot):
        cols = pl.ds(pl.multiple_of(f * TF_MLP, TF_MLP), TF_MLP)
        return (pltpu.make_async_copy(w1_hbm.at[layer, :, cols], w1buf.at[slot], wsem.at[0, slot]),
                pltpu.make_async_copy(w2_hbm.at[layer, cols, :], w2buf.at[slot], wsem.at[1, slot]))

    def start_weights(f, slot):
        for c in weight_copies(f, slot):
            c.start()

    rows = pl.ds(pl.multiple_of(i * TM_MLP, 16), TM_MLP)
    h_copy = pltpu.make_async_copy(h_hbm.at[rows, :], ho_ref, hsem.at[0])
    h_copy.start()

    @pl.when(i == 0)
    def _():
        start_weights(0, 0)

    def step(f, carry):
        slot = lax.rem(f, 2)

        @pl.when(f + 1 < NF_MLP)
        def _():
            start_weights(f + 1, 1 - slot)

        @pl.when(jnp.logical_and(f + 1 == NF_MLP, i + 1 < n_tiles))
        def _():
            start_weights(0, 1 - slot)

        for c in weight_copies(f, slot):
            c.wait()
        mid = jnp.dot(xn_ref[...], w1buf[slot], preferred_element_type=F32)
        mid = jnp.maximum(mid, 0.0)
        act = (mid * mid).astype(BF16)

        @pl.when(f == 0)
        def _():
            h_copy.wait()

        ho_ref[...] += jnp.dot(act, w2buf[slot], preferred_element_type=F32)
        return carry

    lax.fori_loop(0, NF_MLP, step, 0)


def _mlp_kernel(h_hbm, xn_ref, w1_hbm, w2_hbm, g_ref, ho_ref, xo_ref, w1buf, w2buf, wsem, hsem,
                *, layer):
    _mlp_accumulate(h_hbm, xn_ref, w1_hbm, w2_hbm, ho_ref, w1buf, w2buf, wsem, hsem, layer)
    xo_ref[...] = _rms_gain(ho_ref[...], g_ref[...]).astype(xo_ref.dtype)


def _mlp_final_kernel(h_hbm, xn_ref, w1_hbm, w2_hbm, g_ref, ho_ref, out_hbm, w1buf, w2buf, wsem,
                      hsem, obuf, osem, *, layer):
    _mlp_accumulate(h_hbm, xn_ref, w1_hbm, w2_hbm, ho_ref, w1buf, w2buf, wsem, hsem, layer)
    i = pl.program_id(0)
    tiles_per_seq = SEQ_LEN // TM_MLP
    b = i // tiles_per_seq
    ti = i % tiles_per_seq
    body_rows = TM_MLP - N_META
    body = pltpu.make_async_copy(
        obuf.at[pl.ds(N_META, body_rows), :],
        out_hbm.at[b, pl.ds(pl.multiple_of(ti * TM_MLP, 16), body_rows), :], osem.at[0])
    head_start = pl.multiple_of(jnp.maximum(ti * TM_MLP - N_META, 0), 16)
    head = pltpu.make_async_copy(
        obuf.at[pl.ds(0, N_META), :], out_hbm.at[b, pl.ds(head_start, N_META), :], osem.at[1])

    @pl.when(i > 0)
    def _():
        body.wait()

    @pl.when(jnp.logical_and(i > 0, (i - 1) % tiles_per_seq > 0))
    def _():
        head.wait()

    obuf[...] = _rms_gain(ho_ref[...], g_ref[...])
    body.start()

    @pl.when(ti > 0)
    def _():
        head.start()

    @pl.when(i == pl.num_programs(0) - 1)
    def _():
        body.wait()
        head.wait()


def _mlp_call(h, xn, w1, w2, g, layer, final):
    row = lambda i: (i, 0)
    scratch = [pltpu.VMEM((2, D_MODEL, TF_MLP), BF16),
               pltpu.VMEM((2, TF_MLP, D_MODEL), BF16),
               pltpu.SemaphoreType.DMA((2, 2)),
               pltpu.SemaphoreType.DMA((1,))]
    if final:
        body = _mlp_final_kernel
        out_spec = pl.BlockSpec(memory_space=pl.ANY)
        out_shape = jax.ShapeDtypeStruct((BATCH, SEQ, D_MODEL), F32)
        scratch += [pltpu.VMEM((TM_MLP, D_MODEL), F32), pltpu.SemaphoreType.DMA((2,))]
    else:
        body = _mlp_kernel
        out_spec = pl.BlockSpec((TM_MLP, D_MODEL), row)
        out_shape = jax.ShapeDtypeStruct((ROWS, D_MODEL), BF16)
    return pl.pallas_call(
        functools.partial(body, layer=layer),
        grid=(ROWS // TM_MLP,),
        in_specs=[pl.BlockSpec(memory_space=pl.ANY),
                  pl.BlockSpec((TM_MLP, D_MODEL), row),
                  pl.BlockSpec(memory_space=pl.ANY),
                  pl.BlockSpec(memory_space=pl.ANY),
                  pl.BlockSpec((1, D_MODEL), lambda i: (0, 0))],
        out_specs=[pl.BlockSpec((TM_MLP, D_MODEL), row), out_spec],
        out_shape=[jax.ShapeDtypeStruct((ROWS, D_MODEL), F32), out_shape],
        scratch_shapes=scratch,
        compiler_params=_params(("arbitrary",)),
        name="sq_relu_mlp_final" if final else "sq_relu_mlp",
    )(h, xn, w1, w2, g)


def _channel_dft_table():
    c = jnp.arange(GROUP_DIM, dtype=jnp.int32)
    ang = ((c[:, None] * c[None, :]) % GROUP_DIM).astype(F32) * (2.0 * math.pi / GROUP_DIM)
    return jnp.concatenate([jnp.cos(ang), jnp.sin(ang)], axis=1).astype(BF16)


def _sequence_dft_tables():
    scale = 2.0 * math.pi / DFT_LEN
    k = jnp.arange(DFT_K, dtype=jnp.int32)[None, :]
    mh = jnp.arange(DFT_LEN // TABLE_BLOCK, dtype=jnp.int32)[:, None] * TABLE_BLOCK
    ml = jnp.arange(TABLE_BLOCK, dtype=jnp.int32)[:, None]
    ang_h = ((mh * k) % DFT_LEN).astype(F32) * scale
    ang_l = ((ml * k) % DFT_LEN).astype(F32) * scale
    ch, sh = jnp.cos(ang_h)[:, None, :], jnp.sin(ang_h)[:, None, :]
    cl, sl = jnp.cos(ang_l)[None, :, :], jnp.sin(ang_l)[None, :, :]
    live = (k < DFT_LEN)[None]
    cm = jnp.where(live, ch * cl - sh * sl, 0.0).reshape(DFT_LEN, DFT_K).astype(BF16)
    sm = jnp.where(live, sh * cl + ch * sl, 0.0).reshape(DFT_LEN, DFT_K).astype(BF16)
    return cm, sm


def _twiddle_table():
    k = jnp.arange(DFT_K, dtype=jnp.int32)
    cols = []
    for r in range(1, RADIX):
        ang = (r * k).astype(F32) * (2.0 * math.pi / SEQ_LEN)
        cols += [jnp.cos(ang), jnp.sin(ang)]
    cols += [jnp.zeros_like(cols[0])] * (8 - len(cols))
    tw = jnp.stack(cols, axis=1)
    return jnp.where((k < DFT_LEN)[:, None], tw, 0.0)


def _interleave_permutation():
    j = jnp.arange(TM_OUT, dtype=jnp.int32)[:, None]
    c = jnp.arange(TM_OUT, dtype=jnp.int32)[None, :]
    src = (j % RADIX) * DFT_ROWS_PER_TILE + j // RADIX
    return (c == src).astype(BF16)


def kernel(x, meta_tokens, norm_mix_g, norm_mlp_g, norm_final_g, ab_w_in, ab_w_out, ab_conv_w,
           ab_conv_b, ab_ln_g, ab_ln_b, c_w_in, c_conv_w, c_w_out, mlp_w1, mlp_w2):
    depth = norm_mix_g.shape[0]
    meta = jnp.broadcast_to(meta_tokens[None].astype(x.dtype), (BATCH, N_META, D_MODEL))
    h = jnp.concatenate([meta, x], axis=1).reshape(ROWS, D_MODEL)

    cs = _channel_dft_table()
    cm, sm = _sequence_dft_tables()
    tw = _twiddle_table()
    perm = _interleave_permutation()

    ab_w_in, ab_w_out, c_w_in, c_w_out, mlp_w1, mlp_w2 = (
        w.astype(BF16) for w in (ab_w_in, ab_w_out, c_w_in, c_w_out, mlp_w1, mlp_w2))

    xn = _norm_call(h, norm_mix_g[0][None, :])
    for layer in range(depth):
        i = layer // 2
        g_mlp = norm_mlp_g[layer][None, :]
        if layer % 2 == 0:
            uc, us, glu = _even_in_call(xn, ab_w_in, cs, i)
            shape5 = (GROUPS, BATCH, RADIX, DFT_LEN, GROUP_DIM)
            p, q = _butterfly_call(uc.reshape(shape5), us.reshape(shape5), tw)
            ya = _dft_call(cm, sm, p, q)
            cw = jnp.broadcast_to(ab_conv_w[i][:, None, :], (CONV_KERNEL, SUBLANES, MIX_WIDTH))
            h, xn = _even_out_call(h, ya, glu, cw, ab_conv_b[i][None, :], ab_ln_g[i][None, :],
                                   ab_ln_b[i][None, :], perm, ab_w_out, g_mlp, i)
        else:
            gb, p = _odd_in_call(xn, c_w_in, i)
            h, xn = _odd_out_call(h, gb, p, c_conv_w[i], c_w_out, g_mlp, i)
        final = layer == depth - 1
        g_next = norm_final_g if final else norm_mix_g[layer + 1]
        h, xn = _mlp_call(h, xn, mlp_w1, mlp_w2, g_next[None, :], layer, final)
    return xn
```

```python
import functools
import math

import jax
import jax.numpy as jnp
from jax import lax
from jax.experimental import pallas as pl
from jax.experimental.pallas import tpu as pltpu

F32 = jnp.float32
BF16 = jnp.bfloat16

D_MODEL = 2048
BATCH = 2
SEQ = 8192
N_META = 16
SEQ_LEN = SEQ + N_META
ROWS = BATCH * SEQ_LEN
GROUPS = 4
GROUP_DIM = 256
MIX_WIDTH = GROUPS * GROUP_DIM
CONV_KERNEL = 31
CONV_PAD = (CONV_KERNEL - 1) // 2
D_FF = 4 * D_MODEL
EPS = 1e-6

RADIX = 3
DFT_LEN = SEQ_LEN // RADIX
DFT_K = 2816
BFLY_ROWS = DFT_K // 4
DFT_MB = DFT_LEN // 3
TABLE_BLOCK = 48

TM_IN = 912
TM_OUT = 432
TM_MLP = 912
TF_MLP = 1024
NF_MLP = D_FF // TF_MLP
ODD_CHUNK = 512
HALO = 16
SUBLANES = 8
CONV_RC = 48
SHIFT_ROWS = TM_OUT + 2 * HALO - SUBLANES
TILES_PER_SEQ = SEQ_LEN // TM_OUT
DFT_ROWS_PER_TILE = TM_OUT // RADIX
VMEM_LIMIT = 58 * 1024 * 1024

assert SEQ_LEN % TM_IN == 0 and SEQ_LEN % TM_OUT == 0 and ROWS % TM_MLP == 0
assert TM_OUT % CONV_RC == 0 and TM_OUT % HALO == 0 and CONV_RC % 16 == 0
assert DFT_LEN % 16 == 0 and DFT_ROWS_PER_TILE % 16 == 0 and DFT_LEN % TABLE_BLOCK == 0
assert DFT_K % BFLY_ROWS == 0 and BFLY_ROWS % 16 == 0 and NF_MLP % 2 == 0


def _params(sem):
    return pltpu.CompilerParams(dimension_semantics=sem, vmem_limit_bytes=VMEM_LIMIT)


def _rms_gain(x, g):
    ms = jnp.mean(x * x, axis=-1, keepdims=True)
    return x * lax.rsqrt(ms + EPS) * g


def _resident(shape):
    nd = len(shape)
    return pl.BlockSpec(shape, lambda *_: (0,) * nd, pipeline_mode=pl.Buffered(1))


def _resident_layer(shape, layer):
    nd = len(shape)
    return pl.BlockSpec((None,) + shape, lambda *_: (layer,) + (0,) * nd,
                        pipeline_mode=pl.Buffered(1))


def _norm_kernel(h_ref, g_ref, o_ref):
    o_ref[...] = _rms_gain(h_ref[...], g_ref[...]).astype(o_ref.dtype)


def _norm_call(h, g):
    return pl.pallas_call(
        _norm_kernel,
        grid=(ROWS // TM_IN,),
        in_specs=[pl.BlockSpec((TM_IN, D_MODEL), lambda i: (i, 0)),
                  pl.BlockSpec((1, D_MODEL), lambda i: (0, 0))],
        out_specs=pl.BlockSpec((TM_IN, D_MODEL), lambda i: (i, 0)),
        out_shape=jax.ShapeDtypeStruct((ROWS, D_MODEL), BF16),
        compiler_params=_params(("parallel",)),
        name="prologue_norm",
    )(h, g)


def _even_in_kernel(xn_ref, wu_ref, wa_ref, wg_ref, cs_ref, uc_ref, us_ref, glu_ref, wcat):
    wcat[:, 0:GROUP_DIM] = wu_ref[...]
    wcat[:, GROUP_DIM:2 * GROUP_DIM] = wa_ref[...]
    wcat[:, 2 * GROUP_DIM:] = wg_ref[...]
    proj = jnp.dot(xn_ref[...], wcat[...], preferred_element_type=F32)
    u = proj[:, :GROUP_DIM].astype(BF16)
    cs = jnp.dot(u, cs_ref[...], preferred_element_type=F32)
    uc_ref[...] = cs[:, :GROUP_DIM].astype(BF16)
    us_ref[...] = cs[:, GROUP_DIM:].astype(BF16)
    a = proj[:, GROUP_DIM:2 * GROUP_DIM]
    gate = proj[:, 2 * GROUP_DIM:]
    glu_ref[...] = a * jax.nn.sigmoid(gate)


def _even_in_call(xn, w, cs, layer):
    grp = jax.ShapeDtypeStruct((GROUPS, ROWS, GROUP_DIM), BF16)
    wspec = lambda part: pl.BlockSpec((None, D_MODEL, GROUP_DIM),
                                      lambda i, g: (layer, 0, part * GROUPS + g))
    return pl.pallas_call(
        _even_in_kernel,
        grid=(ROWS // TM_IN, GROUPS),
        in_specs=[pl.BlockSpec((TM_IN, D_MODEL), lambda i, g: (i, 0)),
                  wspec(0), wspec(1), wspec(2),
                  pl.BlockSpec((GROUP_DIM, 2 * GROUP_DIM), lambda i, g: (0, 0))],
        out_specs=[pl.BlockSpec((None, TM_IN, GROUP_DIM), lambda i, g: (g, i, 0)),
                   pl.BlockSpec((None, TM_IN, GROUP_DIM), lambda i, g: (g, i, 0)),
                   pl.BlockSpec((TM_IN, GROUP_DIM), lambda i, g: (i, g))],
        out_shape=[grp, grp, jax.ShapeDtypeStruct((ROWS, MIX_WIDTH), F32)],
        scratch_shapes=[pltpu.VMEM((D_MODEL, 3 * GROUP_DIM), BF16)],
        compiler_params=_params(("parallel", "arbitrary")),
        name="even_in_proj",
    )(xn, w, w, w, cs)


def _butterfly_kernel(uc_ref, us_ref, tw_ref, p_ref, q_ref):
    kb = pl.program_id(2)
    half_sqrt3 = 0.5 * math.sqrt(3.0)

    def chunk(c, carry):
        r0 = pl.multiple_of(c * 16, 16)
        rows = pl.ds(r0, 16)
        a = [uc_ref[k, rows, :].astype(F32) for k in range(RADIX)]
        b = [us_ref[k, rows, :].astype(F32) for k in range(RADIX)]
        tw = tw_ref[rows, :]
        row_id = kb * BFLY_ROWS + r0 + lax.broadcasted_iota(jnp.int32, (16, 1), 0)
        valid = row_id < DFT_LEN
        sa, da = a[1] + a[2], a[1] - a[2]
        sb, db = b[1] + b[2], b[1] - b[2]
        xm = a[0] - 0.5 * sa
        ym = 0.5 * sb - b[0]
        xs = [a[0] + sa, xm - half_sqrt3 * db, xm + half_sqrt3 * db]
        ys = [-(b[0] + sb), ym - half_sqrt3 * da, ym + half_sqrt3 * da]
        for r in range(RADIX):
            if r == 0:
                p, q = xs[0], ys[0]
            else:
                cr = tw[:, 2 * r - 2:2 * r - 1]
                sr = tw[:, 2 * r - 1:2 * r]
                p = cr * xs[r] + sr * ys[r]
                q = cr * ys[r] - sr * xs[r]
            lanes = slice(r * GROUP_DIM, (r + 1) * GROUP_DIM)
            p_ref[rows, lanes] = jnp.where(valid, p, 0.0).astype(BF16)
            q_ref[rows, lanes] = jnp.where(valid, q, 0.0).astype(BF16)
        return carry

    lax.fori_loop(0, BFLY_ROWS // 16, chunk, 0, unroll=4)


def _butterfly_call(uc, us, tw):
    in_spec = pl.BlockSpec((None, None, RADIX, BFLY_ROWS, GROUP_DIM),
                           lambda g, b, k: (g, b, 0, k, 0))
    out_spec = pl.BlockSpec((None, None, BFLY_ROWS, RADIX * GROUP_DIM),
                            lambda g, b, k: (g, b, k, 0))
    out = jax.ShapeDtypeStruct((GROUPS, BATCH, DFT_K, RADIX * GROUP_DIM), BF16)
    return pl.pallas_call(
        _butterfly_kernel,
        grid=(GROUPS, BATCH, DFT_K // BFLY_ROWS),
        in_specs=[in_spec, in_spec, pl.BlockSpec((BFLY_ROWS, 8), lambda g, b, k: (k, 0))],
        out_specs=[out_spec, out_spec],
        out_shape=[out, out],
        compiler_params=_params(("parallel", "parallel", "parallel")),
        name="dft_butterfly",
    )(uc, us, tw)


def _dft_kernel(c_ref, s_ref, p_ref, q_ref, o_ref, *, scale):
    acc = jnp.dot(c_ref[...], p_ref[...], preferred_element_type=F32)
    acc = acc + jnp.dot(s_ref[...], q_ref[...], preferred_element_type=F32)
    for r in range(RADIX):
        o_ref[r] = (acc[:, r * GROUP_DIM:(r + 1) * GROUP_DIM] * scale).astype(o_ref.dtype)


def _dft_call(cm, sm, p, q):
    mat_spec = pl.BlockSpec((DFT_MB, DFT_K), lambda m, g, b: (m, 0))
    in_spec = pl.BlockSpec((None, None, DFT_K, RADIX * GROUP_DIM), lambda m, g, b: (g, b, 0, 0))
    return pl.pallas_call(
        functools.partial(_dft_kernel, scale=1.0 / math.sqrt(SEQ_LEN * GROUP_DIM)),
        grid=(DFT_LEN // DFT_MB, GROUPS, BATCH),
        in_specs=[mat_spec, mat_spec, in_spec, in_spec],
        out_specs=pl.BlockSpec((None, None, RADIX, DFT_MB, GROUP_DIM),
                               lambda m, g, b: (g, b, 0, m, 0)),
        out_shape=jax.ShapeDtypeStruct((GROUPS, BATCH, RADIX, DFT_LEN, GROUP_DIM), BF16),
        compiler_params=_params(("parallel", "parallel", "parallel")),
        name="dft_matmul",
    )(cm, sm, p, q)


def _even_out_kernel(h_ref, ya_ref, xm_ref, xp_ref, xnx_ref, cw_ref, cb_ref, lg_ref, lb_ref,
                     perm_ref, w_ref, g_ref, ho_ref, xo_ref, xpad, xshift, yperm, ycat):
    i = pl.program_id(0)
    first = (i % TILES_PER_SEQ) == 0
    last = (i % TILES_PER_SEQ) == TILES_PER_SEQ - 1
    xpad[0:HALO, :] = jnp.where(first, 0.0, xp_ref[...])
    xpad[HALO:HALO + TM_OUT, :] = xm_ref[...]
    xpad[HALO + TM_OUT:, :] = jnp.where(last, 0.0, xnx_ref[...])

    for g in range(GROUPS):
        for r in range(RADIX):
            yperm[r * DFT_ROWS_PER_TILE:(r + 1) * DFT_ROWS_PER_TILE,
                  g * GROUP_DIM:(g + 1) * GROUP_DIM] = ya_ref[g, r]
    ycat[:, :MIX_WIDTH] = jnp.dot(perm_ref[...], yperm[...],
                                  preferred_element_type=F32).astype(BF16)
    ho_ref[...] = h_ref[...] + jnp.dot(ycat[:, :MIX_WIDTH], w_ref[:MIX_WIDTH, :],
                                       preferred_element_type=F32)

    row_vregs = CONV_RC // SUBLANES
    for g in range(GROUPS):
        lanes = slice(g * GROUP_DIM, (g + 1) * GROUP_DIM)
        out_lanes = slice(MIX_WIDTH + g * GROUP_DIM, MIX_WIDTH + (g + 1) * GROUP_DIM)
        for s in range(SUBLANES):
            xshift[g, s] = xpad[s:s + SHIFT_ROWS, lanes]
        for c in range(TM_OUT // CONV_RC):
            r0 = c * CONV_RC
            acc = [jnp.zeros((SUBLANES, GROUP_DIM), F32) for _ in range(row_vregs)]
            for j in range(CONV_KERNEL):
                oq, os_ = divmod(HALO - CONV_PAD + j, SUBLANES)
                w = cw_ref[j, :, lanes]
                for r in range(row_vregs):
                    start = r0 + SUBLANES * (oq + r)
                    acc[r] = acc[r] + w * xshift[g, os_, start:start + SUBLANES, :]
            conv = jnp.concatenate(acc, axis=0) + cb_ref[:, lanes]
            mu = jnp.mean(conv, axis=-1, keepdims=True)
            dev = conv - mu
            var = jnp.mean(dev * dev, axis=-1, keepdims=True)
            y = dev * lax.rsqrt(var + EPS) * lg_ref[:, lanes] + lb_ref[:, lanes]
            y = y * jax.nn.sigmoid(y)
            ycat[r0:r0 + CONV_RC, out_lanes] = y.astype(BF16)
        ho_ref[...] += jnp.dot(ycat[:, out_lanes], w_ref[out_lanes, :],
                               preferred_element_type=F32)

    xo_ref[...] = _rms_gain(ho_ref[...], g_ref[...]).astype(xo_ref.dtype)


def _even_out_call(h, ya, glu, cw, cb, lg, lb, perm, w, g, layer):
    hb = TM_OUT // HALO
    n_halo = ROWS // HALO
    row = lambda i: (i, 0)
    const = lambda i: (0, 0)
    return pl.pallas_call(
        _even_out_kernel,
        grid=(ROWS // TM_OUT,),
        in_specs=[pl.BlockSpec((TM_OUT, D_MODEL), row),
                  pl.BlockSpec((GROUPS, None, RADIX, DFT_ROWS_PER_TILE, GROUP_DIM),
                               lambda i: (0, i // TILES_PER_SEQ, 0, i % TILES_PER_SEQ, 0)),
                  pl.BlockSpec((TM_OUT, MIX_WIDTH), row),
                  pl.BlockSpec((HALO, MIX_WIDTH), lambda i: (jnp.maximum(i * hb - 1, 0), 0)),
                  pl.BlockSpec((HALO, MIX_WIDTH),
                               lambda i: (jnp.minimum((i + 1) * hb, n_halo - 1), 0)),
                  pl.BlockSpec((CONV_KERNEL, SUBLANES, MIX_WIDTH), lambda i: (0, 0, 0)),
                  pl.BlockSpec((1, MIX_WIDTH), const),
                  pl.BlockSpec((1, MIX_WIDTH), const),
                  pl.BlockSpec((1, MIX_WIDTH), const),
                  pl.BlockSpec((TM_OUT, TM_OUT), const),
                  _resident_layer((D_MODEL, D_MODEL), layer),
                  pl.BlockSpec((1, D_MODEL), const)],
        out_specs=[pl.BlockSpec((TM_OUT, D_MODEL), row), pl.BlockSpec((TM_OUT, D_MODEL), row)],
        out_shape=[jax.ShapeDtypeStruct((ROWS, D_MODEL), F32),
                   jax.ShapeDtypeStruct((ROWS, D_MODEL), BF16)],
        scratch_shapes=[pltpu.VMEM((TM_OUT + 2 * HALO, MIX_WIDTH), F32),
                        pltpu.VMEM((GROUPS, SUBLANES, SHIFT_ROWS, GROUP_DIM), F32),
                        pltpu.VMEM((TM_OUT, MIX_WIDTH), BF16),
                        pltpu.VMEM((TM_OUT, D_MODEL), BF16)],
        compiler_params=_params(("parallel",)),
        name="even_out_proj",
    )(h, ya, glu, glu, glu, cw, cb, lg, lb, perm, w, g)


def _odd_in_kernel(xn_ref, wb_ref, wc_ref, wh_ref, gb_ref, p_ref):
    xn = xn_ref[...]
    gb_ref[...] = jnp.dot(xn, wb_ref[...], preferred_element_type=F32).astype(gb_ref.dtype)
    gc = jnp.dot(xn, wc_ref[...], preferred_element_type=F32)
    hin = jnp.dot(xn, wh_ref[...], preferred_element_type=F32)
    p_ref[...] = (gc * hin).astype(p_ref.dtype)


def _odd_in_call(xn, w, layer):
    n_chunks = D_MODEL // ODD_CHUNK
    out = jax.ShapeDtypeStruct((ROWS, D_MODEL), BF16)
    wspec = lambda part: pl.BlockSpec((None, D_MODEL, ODD_CHUNK),
                                      lambda i, n: (layer, 0, part * n_chunks + n))
    return pl.pallas_call(
        _odd_in_kernel,
        grid=(ROWS // TM_IN, n_chunks),
        in_specs=[pl.BlockSpec((TM_IN, D_MODEL), lambda i, n: (i, 0)),
                  wspec(0), wspec(1), wspec(2)],
        out_specs=[pl.BlockSpec((TM_IN, ODD_CHUNK), lambda i, n: (i, n)),
                   pl.BlockSpec((TM_IN, ODD_CHUNK), lambda i, n: (i, n))],
        out_shape=[out, out],
        compiler_params=_params(("parallel", "arbitrary")),
        name="odd_in_proj",
    )(xn, w, w, w)


def _odd_out_kernel(h_ref, gb_ref, pm_ref, pp_ref, pn_ref, cw_ref, w_ref, g_ref,
                    ho_ref, xo_ref, xmid, xprev, xnext, ybuf):
    i = pl.program_id(0)
    first = (i % TILES_PER_SEQ) == 0
    last = (i % TILES_PER_SEQ) == TILES_PER_SEQ - 1
    xmid[...] = pm_ref[...].astype(F32)
    xprev[0:1, :] = jnp.where(first, 0.0, pp_ref[HALO - 1:HALO, :].astype(F32))
    xprev[1:TM_OUT, :] = xmid[0:TM_OUT - 1, :]
    xnext[0:TM_OUT - 1, :] = xmid[1:TM_OUT, :]
    xnext[TM_OUT - 1:TM_OUT, :] = jnp.where(last, 0.0, pn_ref[0:1, :].astype(F32))

    n_blocks = 4
    width = D_MODEL // n_blocks
    for blk in range(n_blocks):
        lanes = slice(blk * width, (blk + 1) * width)
        for c in range(TM_OUT // 16):
            rows = slice(c * 16, (c + 1) * 16)
            conv = cw_ref[0:1, lanes] * xprev[rows, lanes]
            conv = conv + cw_ref[1:2, lanes] * xmid[rows, lanes]
            conv = conv + cw_ref[2:3, lanes] * xnext[rows, lanes]
            ybuf[rows, lanes] = (gb_ref[rows, lanes].astype(F32) * conv).astype(BF16)
        part = jnp.dot(ybuf[:, lanes], w_ref[lanes, :], preferred_element_type=F32)
        if blk == 0:
            ho_ref[...] = h_ref[...] + part
        else:
            ho_ref[...] += part

    xo_ref[...] = _rms_gain(ho_ref[...], g_ref[...]).astype(xo_ref.dtype)


def _odd_out_call(h, gb, p, cw, w, g, layer):
    hb = TM_OUT // HALO
    n_halo = ROWS // HALO
    row = lambda i: (i, 0)
    const = lambda i: (0, 0)
    return pl.pallas_call(
        _odd_out_kernel,
        grid=(ROWS // TM_OUT,),
        in_specs=[pl.BlockSpec((TM_OUT, D_MODEL), row),
                  pl.BlockSpec((TM_OUT, D_MODEL), row),
                  pl.BlockSpec((TM_OUT, D_MODEL), row),
                  pl.BlockSpec((HALO, D_MODEL), lambda i: (jnp.maximum(i * hb - 1, 0), 0)),
                  pl.BlockSpec((HALO, D_MODEL),
                               lambda i: (jnp.minimum((i + 1) * hb, n_halo - 1), 0)),
                  pl.BlockSpec((3, D_MODEL), const),
                  _resident_layer((D_MODEL, D_MODEL), layer),
                  pl.BlockSpec((1, D_MODEL), const)],
        out_specs=[pl.BlockSpec((TM_OUT, D_MODEL), row), pl.BlockSpec((TM_OUT, D_MODEL), row)],
        out_shape=[jax.ShapeDtypeStruct((ROWS, D_MODEL), F32),
                   jax.ShapeDtypeStruct((ROWS, D_MODEL), BF16)],
        scratch_shapes=[pltpu.VMEM((TM_OUT, D_MODEL), F32),
                        pltpu.VMEM((TM_OUT, D_MODEL), F32),
                        pltpu.VMEM((TM_OUT, D_MODEL), F32),
                        pltpu.VMEM((TM_OUT, D_MODEL), BF16)],
        compiler_params=_params(("parallel",)),
        name="odd_out_proj",
    )(h, gb, p, p, p, cw, w, g)


def _mlp_accumulate(h_hbm, xn_ref, w1_hbm, w2_hbm, ho_ref, w1buf, w2buf, wsem, hsem, layer):
    i = pl.program_id(0)
    n_tiles = pl.num_programs(0)

    def weight_copies(f, slot):
        cols = pl.ds(pl.multiple_of(f * TF_MLP, TF_MLP), TF_MLP)
        return (pltpu.make_async_copy(w1_hbm.at[layer, :, cols], w1buf.at[slot], wsem.at[0, slot]),
                pltpu.make_async_copy(w2_hbm.at[layer, cols, :], w2buf.at[slot], wsem.at[1, slot]))

    def start_weights(f, slot):
        for c in weight_copies(f, slot):
            c.start()

    rows = pl.ds(pl.multiple_of(i * TM_MLP, 16), TM_MLP)
    h_copy = pltpu.make_async_copy(h_hbm.at[rows, :], ho_ref, hsem.at[0])
    h_copy.start()

    @pl.when(i == 0)
    def _():
        start_weights(0, 0)

    def step(f, carry):
        slot = lax.rem(f, 2)

        @pl.when(f + 1 < NF_MLP)
        def _():
            start_weights(f + 1, 1 - slot)

        @pl.when(jnp.logical_and(f + 1 == NF_MLP, i + 1 < n_tiles))
        def _():
            start_weights(0, 1 - slot)

        for c in weight_copies(f, slot):
            c.wait()
        mid = jnp.dot(xn_ref[...], w1buf[slot], preferred_element_type=F32)
        mid = jnp.maximum(mid, 0.0)
        act = (mid * mid).astype(BF16)

        @pl.when(f == 0)
        def _():
            h_copy.wait()

        ho_ref[...] += jnp.dot(act, w2buf[slot], preferred_element_type=F32)
        return carry

    lax.fori_loop(0, NF_MLP, step, 0)


def _mlp_kernel(h_hbm, xn_ref, w1_hbm, w2_hbm, g_ref, ho_ref, xo_ref, w1buf, w2buf, wsem, hsem,
                *, layer):
    _mlp_accumulate(h_hbm, xn_ref, w1_hbm, w2_hbm, ho_ref, w1buf, w2buf, wsem, hsem, layer)
    xo_ref[...] = _rms_gain(ho_ref[...], g_ref[...]).astype(xo_ref.dtype)


def _mlp_final_kernel(h_hbm, xn_ref, w1_hbm, w2_hbm, g_ref, ho_ref, out_hbm, w1buf, w2buf, wsem,
                      hsem, obuf, osem, *, layer):
    _mlp_accumulate(h_hbm, xn_ref, w1_hbm, w2_hbm, ho_ref, w1buf, w2buf, wsem, hsem, layer)
    i = pl.program_id(0)
    tiles_per_seq = SEQ_LEN // TM_MLP
    b = i // tiles_per_seq
    ti = i % tiles_per_seq
    body_rows = TM_MLP - N_META
    body = pltpu.make_async_copy(
        obuf.at[pl.ds(N_META, body_rows), :],
        out_hbm.at[b, pl.ds(pl.multiple_of(ti * TM_MLP, 16), body_rows), :], osem.at[0])
    head_start = pl.multiple_of(jnp.maximum(ti * TM_MLP - N_META, 0), 16)
    head = pltpu.make_async_copy(
        obuf.at[pl.ds(0, N_META), :], out_hbm.at[b, pl.ds(head_start, N_META), :], osem.at[1])

    @pl.when(i > 0)
    def _():
        body.wait()

    @pl.when(jnp.logical_and(i > 0, (i - 1) % tiles_per_seq > 0))
    def _():
        head.wait()

    obuf[...] = _rms_gain(ho_ref[...], g_ref[...])
    body.start()

    @pl.when(ti > 0)
    def _():
        head.start()

    @pl.when(i == pl.num_programs(0) - 1)
    def _():
        body.wait()
        head.wait()


def _mlp_call(h, xn, w1, w2, g, layer, final):
    row = lambda i: (i, 0)
    scratch = [pltpu.VMEM((2, D_MODEL, TF_MLP), BF16),
               pltpu.VMEM((2, TF_MLP, D_MODEL), BF16),
               pltpu.SemaphoreType.DMA((2, 2)),
               pltpu.SemaphoreType.DMA((1,))]
    if final:
        body = _mlp_final_kernel
        out_spec = pl.BlockSpec(memory_space=pl.ANY)
        out_shape = jax.ShapeDtypeStruct((BATCH, SEQ, D_MODEL), F32)
        scratch += [pltpu.VMEM((TM_MLP, D_MODEL), F32), pltpu.SemaphoreType.DMA((2,))]
    else:
        body = _mlp_kernel
        out_spec = pl.BlockSpec((TM_MLP, D_MODEL), row)
        out_shape = jax.ShapeDtypeStruct((ROWS, D_MODEL), BF16)
    return pl.pallas_call(
        functools.partial(body, layer=layer),
        grid=(ROWS // TM_MLP,),
        in_specs=[pl.BlockSpec(memory_space=pl.ANY),
                  pl.BlockSpec((TM_MLP, D_MODEL), row),
                  pl.BlockSpec(memory_space=pl.ANY),
                  pl.BlockSpec(memory_space=pl.ANY),
                  pl.BlockSpec((1, D_MODEL), lambda i: (0, 0))],
        out_specs=[pl.BlockSpec((TM_MLP, D_MODEL), row), out_spec],
        out_shape=[jax.ShapeDtypeStruct((ROWS, D_MODEL), F32), out_shape],
        scratch_shapes=scratch,
        compiler_params=_params(("arbitrary",)),
        name="sq_relu_mlp_final" if final else "sq_relu_mlp",
    )(h, xn, w1, w2, g)


def _channel_dft_table():
    c = jnp.arange(GROUP_DIM, dtype=jnp.int32)
    ang = ((c[:, None] * c[None, :]) % GROUP_DIM).astype(F32) * (2.0 * math.pi / GROUP_DIM)
    return jnp.concatenate([jnp.cos(ang), jnp.sin(ang)], axis=1).astype(BF16)


def _sequence_dft_tables():
    scale = 2.0 * math.pi / DFT_LEN
    k = jnp.arange(DFT_K, dtype=jnp.int32)[None, :]
    mh = jnp.arange(DFT_LEN // TABLE_BLOCK, dtype=jnp.int32)[:, None] * TABLE_BLOCK
    ml = jnp.arange(TABLE_BLOCK, dtype=jnp.int32)[:, None]
    ang_h = ((mh * k) % DFT_LEN).astype(F32) * scale
    ang_l = ((ml * k) % DFT_LEN).astype(F32) * scale
    ch, sh = jnp.cos(ang_h)[:, None, :], jnp.sin(ang_h)[:, None, :]
    cl, sl = jnp.cos(ang_l)[None, :, :], jnp.sin(ang_l)[None, :, :]
    live = (k < DFT_LEN)[None]
    cm = jnp.where(live, ch * cl - sh * sl, 0.0).reshape(DFT_LEN, DFT_K).astype(BF16)
    sm = jnp.where(live, sh * cl + ch * sl, 0.0).reshape(DFT_LEN, DFT_K).astype(BF16)
    return cm, sm


def _twiddle_table():
    k = jnp.arange(DFT_K, dtype=jnp.int32)
    cols = []
    for r in range(1, RADIX):
        ang = (r * k).astype(F32) * (2.0 * math.pi / SEQ_LEN)
        cols += [jnp.cos(ang), jnp.sin(ang)]
    cols += [jnp.zeros_like(cols[0])] * (8 - len(cols))
    tw = jnp.stack(cols, axis=1)
    return jnp.where((k < DFT_LEN)[:, None], tw, 0.0)


def _interleave_permutation():
    j = jnp.arange(TM_OUT, dtype=jnp.int32)[:, None]
    c = jnp.arange(TM_OUT, dtype=jnp.int32)[None, :]
    src = (j % RADIX) * DFT_ROWS_PER_TILE + j // RADIX
    return (c == src).astype(BF16)


def kernel(x, meta_tokens, norm_mix_g, norm_mlp_g, norm_final_g, ab_w_in, ab_w_out, ab_conv_w,
           ab_conv_b, ab_ln_g, ab_ln_b, c_w_in, c_conv_w, c_w_out, mlp_w1, mlp_w2):
    depth = norm_mix_g.shape[0]
    meta = jnp.broadcast_to(meta_tokens[None].astype(x.dtype), (BATCH, N_META, D_MODEL))
    h = jnp.concatenate([meta, x], axis=1).reshape(ROWS, D_MODEL)

    cs = _channel_dft_table()
    cm, sm = _sequence_dft_tables()
    tw = _twiddle_table()
    perm = _interleave_permutation()

    ab_w_in, ab_w_out, c_w_in, c_w_out, mlp_w1, mlp_w2 = (
        w.astype(BF16) for w in (ab_w_in, ab_w_out, c_w_in, c_w_out, mlp_w1, mlp_w2))

    xn = _norm_call(h, norm_mix_g[0][None, :])
    for layer in range(depth):
        i = layer // 2
        g_mlp = norm_mlp_g[layer][None, :]
        if layer % 2 == 0:
            uc, us, glu = _even_in_call(xn, ab_w_in, cs, i)
            shape5 = (GROUPS, BATCH, RADIX, DFT_LEN, GROUP_DIM)
            p, q = _butterfly_call(uc.reshape(shape5), us.reshape(shape5), tw)
            ya = _dft_call(cm, sm, p, q)
            cw = jnp.broadcast_to(ab_conv_w[i][:, None, :], (CONV_KERNEL, SUBLANES, MIX_WIDTH))
            h, xn = _even_out_call(h, ya, glu, cw, ab_conv_b[i][None, :], ab_ln_g[i][None, :],
                                   ab_ln_b[i][None, :], perm, ab_w_out, g_mlp, i)
        else:
            gb, p = _odd_in_call(xn, c_w_in, i)
            h, xn = _odd_out_call(h, gb, p, c_conv_w[i], c_w_out, g_mlp, i)
        final = layer == depth - 1
        g_next = norm_final_g if final else norm_mix_g[layer + 1]
        h, xn = _mlp_call(h, xn, mlp_w1, mlp_w2, g_next[None, :], layer, final)
    return xn
```

```python
import functools
import math

import jax
import jax.numpy as jnp
from jax import lax
from jax.experimental import pallas as pl
from jax.experimental.pallas import tpu as pltpu

F32 = jnp.float32
BF16 = jnp.bfloat16

D_MODEL = 2048
BATCH = 2
SEQ = 8192
N_META = 16
SEQ_LEN = SEQ + N_META
ROWS = BATCH * SEQ_LEN
GROUPS = 4
GROUP_DIM = 256
MIX_WIDTH = GROUPS * GROUP_DIM
CONV_KERNEL = 31
CONV_PAD = (CONV_KERNEL - 1) // 2
D_FF = 4 * D_MODEL
EPS = 1e-6

RADIX = 3
DFT_LEN = SEQ_LEN // RADIX
DFT_K = 2816
BFLY_ROWS = DFT_K // 4
DFT_MB = DFT_LEN // 3
TABLE_BLOCK = 48

TM_IN = 912
TM_OUT = 432
TM_MLP = 912
TF_MLP = 1024
NF_MLP = D_FF // TF_MLP
ODD_CHUNK = 512
HALO = 16
SUBLANES = 8
CONV_RC = 48
SHIFT_ROWS = TM_OUT + 2 * HALO - SUBLANES
TILES_PER_SEQ = SEQ_LEN // TM_OUT
DFT_ROWS_PER_TILE = TM_OUT // RADIX
VMEM_LIMIT = 58 * 1024 * 1024

assert SEQ_LEN % TM_IN == 0 and SEQ_LEN % TM_OUT == 0 and ROWS % TM_MLP == 0
assert TM_OUT % CONV_RC == 0 and TM_OUT % HALO == 0 and CONV_RC % 16 == 0
assert DFT_LEN % 16 == 0 and DFT_ROWS_PER_TILE % 16 == 0 and DFT_LEN % TABLE_BLOCK == 0
assert DFT_K % BFLY_ROWS == 0 and BFLY_ROWS % 16 == 0 and NF_MLP % 2 == 0


def _params(sem):
    return pltpu.CompilerParams(dimension_semantics=sem, vmem_limit_bytes=VMEM_LIMIT)


def _rms_gain(x, g):
    ms = jnp.mean(x * x, axis=-1, keepdims=True)
    return x * lax.rsqrt(ms + EPS) * g


def _resident(shape):
    nd = len(shape)
    return pl.BlockSpec(shape, lambda *_: (0,) * nd, pipeline_mode=pl.Buffered(1))


def _resident_layer(shape, layer):
    nd = len(shape)
    return pl.BlockSpec((None,) + shape, lambda *_: (layer,) + (0,) * nd,
                        pipeline_mode=pl.Buffered(1))


def _norm_kernel(h_ref, g_ref, o_ref):
    o_ref[...] = _rms_gain(h_ref[...], g_ref[...]).astype(o_ref.dtype)


def _norm_call(h, g):
    return pl.pallas_call(
        _norm_kernel,
        grid=(ROWS // TM_IN,),
        in_specs=[pl.BlockSpec((TM_IN, D_MODEL), lambda i: (i, 0)),
                  pl.BlockSpec((1, D_MODEL), lambda i: (0, 0))],
        out_specs=pl.BlockSpec((TM_IN, D_MODEL), lambda i: (i, 0)),
        out_shape=jax.ShapeDtypeStruct((ROWS, D_MODEL), BF16),
        compiler_params=_params(("parallel",)),
        name="prologue_norm",
    )(h, g)


def _even_in_kernel(xn_ref, wu_ref, wa_ref, wg_ref, cs_ref, uc_ref, us_ref, glu_ref, wcat):
    wcat[:, 0:GROUP_DIM] = wu_ref[...]
    wcat[:, GROUP_DIM:2 * GROUP_DIM] = wa_ref[...]
    wcat[:, 2 * GROUP_DIM:] = wg_ref[...]
    proj = jnp.dot(xn_ref[...], wcat[...], preferred_element_type=F32)
    u = proj[:, :GROUP_DIM].astype(BF16)
    cs = jnp.dot(u, cs_ref[...], preferred_element_type=F32)
    uc_ref[...] = cs[:, :GROUP_DIM].astype(BF16)
    us_ref[...] = cs[:, GROUP_DIM:].astype(BF16)
    a = proj[:, GROUP_DIM:2 * GROUP_DIM]
    gate = proj[:, 2 * GROUP_DIM:]
    glu_ref[...] = a * jax.nn.sigmoid(gate)


def _even_in_call(xn, w, cs, layer):
    grp = jax.ShapeDtypeStruct((GROUPS, ROWS, GROUP_DIM), BF16)
    wspec = lambda part: pl.BlockSpec((None, D_MODEL, GROUP_DIM),
                                      lambda i, g: (layer, 0, part * GROUPS + g))
    return pl.pallas_call(
        _even_in_kernel,
        grid=(ROWS // TM_IN, GROUPS),
        in_specs=[pl.BlockSpec((TM_IN, D_MODEL), lambda i, g: (i, 0)),
                  wspec(0), wspec(1), wspec(2),
                  pl.BlockSpec((GROUP_DIM, 2 * GROUP_DIM), lambda i, g: (0, 0))],
        out_specs=[pl.BlockSpec((None, TM_IN, GROUP_DIM), lambda i, g: (g, i, 0)),
                   pl.BlockSpec((None, TM_IN, GROUP_DIM), lambda i, g: (g, i, 0)),
                   pl.BlockSpec((TM_IN, GROUP_DIM), lambda i, g: (i, g))],
        out_shape=[grp, grp, jax.ShapeDtypeStruct((ROWS, MIX_WIDTH), F32)],
        scratch_shapes=[pltpu.VMEM((D_MODEL, 3 * GROUP_DIM), BF16)],
        compiler_params=_params(("parallel", "arbitrary")),
        name="even_in_proj",
    )(xn, w, w, w, cs)


def _butterfly_kernel(uc_ref, us_ref, tw_ref, p_ref, q_ref):
    kb = pl.program_id(2)
    half_sqrt3 = 0.5 * math.sqrt(3.0)

    def chunk(c, carry):
        r0 = pl.multiple_of(c * 16, 16)
        rows = pl.ds(r0, 16)
        a = [uc_ref[k, rows, :].astype(F32) for k in range(RADIX)]
        b = [us_ref[k, rows, :].astype(F32) for k in range(RADIX)]
        tw = tw_ref[rows, :]
        row_id = kb * BFLY_ROWS + r0 + lax.broadcasted_iota(jnp.int32, (16, 1), 0)
        valid = row_id < DFT_LEN
        sa, da = a[1] + a[2], a[1] - a[2]
        sb, db = b[1] + b[2], b[1] - b[2]
        xm = a[0] - 0.5 * sa
        ym = 0.5 * sb - b[0]
        xs = [a[0] + sa, xm - half_sqrt3 * db, xm + half_sqrt3 * db]
        ys = [-(b[0] + sb), ym - half_sqrt3 * da, ym + half_sqrt3 * da]
        for r in range(RADIX):
            if r == 0:
                p, q = xs[0], ys[0]
            else:
                cr = tw[:, 2 * r - 2:2 * r - 1]
                sr = tw[:, 2 * r - 1:2 * r]
                p = cr * xs[r] + sr * ys[r]
                q = cr * ys[r] - sr * xs[r]
            lanes = slice(r * GROUP_DIM, (r + 1) * GROUP_DIM)
            p_ref[rows, lanes] = jnp.where(valid, p, 0.0).astype(BF16)
            q_ref[rows, lanes] = jnp.where(valid, q, 0.0).astype(BF16)
        return carry

    lax.fori_loop(0, BFLY_ROWS // 16, chunk, 0, unroll=4)


def _butterfly_call(uc, us, tw):
    in_spec = pl.BlockSpec((None, None, RADIX, BFLY_ROWS, GROUP_DIM),
                           lambda g, b, k: (g, b, 0, k, 0))
    out_spec = pl.BlockSpec((None, None, BFLY_ROWS, RADIX * GROUP_DIM),
                            lambda g, b, k: (g, b, k, 0))
    out = jax.ShapeDtypeStruct((GROUPS, BATCH, DFT_K, RADIX * GROUP_DIM), BF16)
    return pl.pallas_call(
        _butterfly_kernel,
        grid=(GROUPS, BATCH, DFT_K // BFLY_ROWS),
        in_specs=[in_spec, in_spec, pl.BlockSpec((BFLY_ROWS, 8), lambda g, b, k: (k, 0))],
        out_specs=[out_spec, out_spec],
        out_shape=[out, out],
        compiler_params=_params(("parallel", "parallel", "parallel")),
        name="dft_butterfly",
    )(uc, us, tw)


def _dft_kernel(c_ref, s_ref, p_ref, q_ref, o_ref, *, scale):
    acc = jnp.dot(c_ref[...], p_ref[...], preferred_element_type=F32)
    acc = acc + jnp.dot(s_ref[...], q_ref[...], preferred_element_type=F32)
    for r in range(RADIX):
        o_ref[r] = (acc[:, r * GROUP_DIM:(r + 1) * GROUP_DIM] * scale).astype(o_ref.dtype)


def _dft_call(cm, sm, p, q):
    mat_spec = pl.BlockSpec((DFT_MB, DFT_K), lambda m, g, b: (m, 0))
    in_spec = pl.BlockSpec((None, None, DFT_K, RADIX * GROUP_DIM), lambda m, g, b: (g, b, 0, 0))
    return pl.pallas_call(
        functools.partial(_dft_kernel, scale=1.0 / math.sqrt(SEQ_LEN * GROUP_DIM)),
        grid=(DFT_LEN // DFT_MB, GROUPS, BATCH),
        in_specs=[mat_spec, mat_spec, in_spec, in_spec],
        out_specs=pl.BlockSpec((None, None, RADIX, DFT_MB, GROUP_DIM),
                               lambda m, g, b: (g, b, 0, m, 0)),
        out_shape=jax.ShapeDtypeStruct((GROUPS, BATCH, RADIX, DFT_LEN, GROUP_DIM), BF16),
        compiler_params=_params(("parallel", "parallel", "parallel")),
        name="dft_matmul",
    )(cm, sm, p, q)


def _even_out_kernel(h_ref, ya_ref, xm_ref, xp_ref, xnx_ref, cw_ref, cb_ref, lg_ref, lb_ref,
                     perm_ref, w_ref, g_ref, ho_ref, xo_ref, xpad, xshift, yperm, ycat):
    i = pl.program_id(0)
    first = (i % TILES_PER_SEQ) == 0
    last = (i % TILES_PER_SEQ) == TILES_PER_SEQ - 1
    xpad[0:HALO, :] = jnp.where(first, 0.0, xp_ref[...])
    xpad[HALO:HALO + TM_OUT, :] = xm_ref[...]
    xpad[HALO + TM_OUT:, :] = jnp.where(last, 0.0, xnx_ref[...])

    for g in range(GROUPS):
        for r in range(RADIX):
            yperm[r * DFT_ROWS_PER_TILE:(r + 1) * DFT_ROWS_PER_TILE,
                  g * GROUP_DIM:(g + 1) * GROUP_DIM] = ya_ref[g, r]
    ycat[:, :MIX_WIDTH] = jnp.dot(perm_ref[...], yperm[...],
                                  preferred_element_type=F32).astype(BF16)
    ho_ref[...] = h_ref[...] + jnp.dot(ycat[:, :MIX_WIDTH], w_ref[:MIX_WIDTH, :],
                                       preferred_element_type=F32)

    row_vregs = CONV_RC // SUBLANES
    for g in range(GROUPS):
        lanes = slice(g * GROUP_DIM, (g + 1) * GROUP_DIM)
        out_lanes = slice(MIX_WIDTH + g * GROUP_DIM, MIX_WIDTH + (g + 1) * GROUP_DIM)
        for s in range(SUBLANES):
            xshift[g, s] = xpad[s:s + SHIFT_ROWS, lanes]
        for c in range(TM_OUT // CONV_RC):
            r0 = c * CONV_RC
            acc = [jnp.zeros((SUBLANES, GROUP_DIM), F32) for _ in range(row_vregs)]
            for j in range(CONV_KERNEL):
                oq, os_ = divmod(HALO - CONV_PAD + j, SUBLANES)
                w = cw_ref[j, :, lanes]
                for r in range(row_vregs):
                    start = r0 + SUBLANES * (oq + r)
                    acc[r] = acc[r] + w * xshift[g, os_, start:start + SUBLANES, :]
            conv = jnp.concatenate(acc, axis=0) + cb_ref[:, lanes]
            mu = jnp.mean(conv, axis=-1, keepdims=True)
            dev = conv - mu
            var = jnp.mean(dev * dev, axis=-1, keepdims=True)
            y = dev * lax.rsqrt(var + EPS) * lg_ref[:, lanes] + lb_ref[:, lanes]
            y = y * jax.nn.sigmoid(y)
            ycat[r0:r0 + CONV_RC, out_lanes] = y.astype(BF16)
        ho_ref[...] += jnp.dot(ycat[:, out_lanes], w_ref[out_lanes, :],
                               preferred_element_type=F32)

    xo_ref[...] = _rms_gain(ho_ref[...], g_ref[...]).astype(xo_ref.dtype)


def _even_out_call(h, ya, glu, cw, cb, lg, lb, perm, w, g, layer):
    hb = TM_OUT // HALO
    n_halo = ROWS // HALO
    row = lambda i: (i, 0)
    const = lambda i: (0, 0)
    return pl.pallas_call(
        _even_out_kernel,
        grid=(ROWS // TM_OUT,),
        in_specs=[pl.BlockSpec((TM_OUT, D_MODEL), row),
                  pl.BlockSpec((GROUPS, None, RADIX, DFT_ROWS_PER_TILE, GROUP_DIM),
                               lambda i: (0, i // TILES_PER_SEQ, 0, i % TILES_PER_SEQ, 0)),
                  pl.BlockSpec((TM_OUT, MIX_WIDTH), row),
                  pl.BlockSpec((HALO, MIX_WIDTH), lambda i: (jnp.maximum(i * hb - 1, 0), 0)),
                  pl.BlockSpec((HALO, MIX_WIDTH),
                               lambda i: (jnp.minimum((i + 1) * hb, n_halo - 1), 0)),
                  pl.BlockSpec((CONV_KERNEL, SUBLANES, MIX_WIDTH), lambda i: (0, 0, 0)),
                  pl.BlockSpec((1, MIX_WIDTH), const),
                  pl.BlockSpec((1, MIX_WIDTH), const),
                  pl.BlockSpec((1, MIX_WIDTH), const),
                  pl.BlockSpec((TM_OUT, TM_OUT), const),
                  _resident_layer((D_MODEL, D_MODEL), layer),
                  pl.BlockSpec((1, D_MODEL), const)],
        out_specs=[pl.BlockSpec((TM_OUT, D_MODEL), row), pl.BlockSpec((TM_OUT, D_MODEL), row)],
        out_shape=[jax.ShapeDtypeStruct((ROWS, D_MODEL), F32),
                   jax.ShapeDtypeStruct((ROWS, D_MODEL), BF16)],
        scratch_shapes=[pltpu.VMEM((TM_OUT + 2 * HALO, MIX_WIDTH), F32),
                        pltpu.VMEM((GROUPS, SUBLANES, SHIFT_ROWS, GROUP_DIM), F32),
                        pltpu.VMEM((TM_OUT, MIX_WIDTH), BF16),
                        pltpu.VMEM((TM_OUT, D_MODEL), BF16)],
        compiler_params=_params(("parallel",)),
        name="even_out_proj",
    )(h, ya, glu, glu, glu, cw, cb, lg, lb, perm, w, g)


def _odd_in_kernel(xn_ref, wb_ref, wc_ref, wh_ref, gb_ref, p_ref):
    xn = xn_ref[...]
    gb_ref[...] = jnp.dot(xn, wb_ref[...], preferred_element_type=F32).astype(gb_ref.dtype)
    gc = jnp.dot(xn, wc_ref[...], preferred_element_type=F32)
    hin = jnp.dot(xn, wh_ref[...], preferred_element_type=F32)
    p_ref[...] = (gc * hin).astype(p_ref.dtype)


def _odd_in_call(xn, w, layer):
    n_chunks = D_MODEL // ODD_CHUNK
    out = jax.ShapeDtypeStruct((ROWS, D_MODEL), BF16)
    wspec = lambda part: pl.BlockSpec((None, D_MODEL, ODD_CHUNK),
                                      lambda i, n: (layer, 0, part * n_chunks + n))
    return pl.pallas_call(
        _odd_in_kernel,
        grid=(ROWS // TM_IN, n_chunks),
        in_specs=[pl.BlockSpec((TM_IN, D_MODEL), lambda i, n: (i, 0)),
                  wspec(0), wspec(1), wspec(2)],
        out_specs=[pl.BlockSpec((TM_IN, ODD_CHUNK), lambda i, n: (i, n)),
                   pl.BlockSpec((TM_IN, ODD_CHUNK), lambda i, n: (i, n))],
        out_shape=[out, out],
        compiler_params=_params(("parallel", "arbitrary")),
        name="odd_in_proj",
    )(xn, w, w, w)


def _odd_out_kernel(h_ref, gb_ref, pm_ref, pp_ref, pn_ref, cw_ref, w_ref, g_ref,
                    ho_ref, xo_ref, xmid, xprev, xnext, ybuf):
    i = pl.program_id(0)
    first = (i % TILES_PER_SEQ) == 0
    last = (i % TILES_PER_SEQ) == TILES_PER_SEQ - 1
    xmid[...] = pm_ref[...].astype(F32)
    xprev[0:1, :] = jnp.where(first, 0.0, pp_ref[HALO - 1:HALO, :].astype(F32))
    xprev[1:TM_OUT, :] = xmid[0:TM_OUT - 1, :]
    xnext[0:TM_OUT - 1, :] = xmid[1:TM_OUT, :]
    xnext[TM_OUT - 1:TM_OUT, :] = jnp.where(last, 0.0, pn_ref[0:1, :].astype(F32))

    n_blocks = 4
    width = D_MODEL // n_blocks
    for blk in range(n_blocks):
        lanes = slice(blk * width, (blk + 1) * width)
        for c in range(TM_OUT // 16):
            rows = slice(c * 16, (c + 1) * 16)
            conv = cw_ref[0:1, lanes] * xprev[rows, lanes]
            conv = conv + cw_ref[1:2, lanes] * xmid[rows, lanes]
            conv = conv + cw_ref[2:3, lanes] * xnext[rows, lanes]
            ybuf[rows, lanes] = (gb_ref[rows, lanes].astype(F32) * conv).astype(BF16)
        part = jnp.dot(ybuf[:, lanes], w_ref[lanes, :], preferred_element_type=F32)
        if blk == 0:
            ho_ref[...] = h_ref[...] + part
        else:
            ho_ref[...] += part

    xo_ref[...] = _rms_gain(ho_ref[...], g_ref[...]).astype(xo_ref.dtype)


def _odd_out_call(h, gb, p, cw, w, g, layer):
    hb = TM_OUT // HALO
    n_halo = ROWS // HALO
    row = lambda i: (i, 0)
    const = lambda i: (0, 0)
    return pl.pallas_call(
        _odd_out_kernel,
        grid=(ROWS // TM_OUT,),
        in_specs=[pl.BlockSpec((TM_OUT, D_MODEL), row),
                  pl.BlockSpec((TM_OUT, D_MODEL), row),
                  pl.BlockSpec((TM_OUT, D_MODEL), row),
                  pl.BlockSpec((HALO, D_MODEL), lambda i: (jnp.maximum(i * hb - 1, 0), 0)),
                  pl.BlockSpec((HALO, D_MODEL),
                               lambda i: (jnp.minimum((i + 1) * hb, n_halo - 1), 0)),
                  pl.BlockSpec((3, D_MODEL), const),
                  _resident_layer((D_MODEL, D_MODEL), layer),
                  pl.BlockSpec((1, D_MODEL), const)],
        out_specs=[pl.BlockSpec((TM_OUT, D_MODEL), row), pl.BlockSpec((TM_OUT, D_MODEL), row)],
        out_shape=[jax.ShapeDtypeStruct((ROWS, D_MODEL), F32),
                   jax.ShapeDtypeStruct((ROWS, D_MODEL), BF16)],
        scratch_shapes=[pltpu.VMEM((TM_OUT, D_MODEL), F32),
                        pltpu.VMEM((TM_OUT, D_MODEL), F32),
                        pltpu.VMEM((TM_OUT, D_MODEL), F32),
                        pltpu.VMEM((TM_OUT, D_MODEL), BF16)],
        compiler_params=_params(("parallel",)),
        name="odd_out_proj",
    )(h, gb, p, p, p, cw, w, g)


def _mlp_accumulate(h_hbm, xn_ref, w1_hbm, w2_hbm, acc_ref, w1buf, w2buf, actbuf, wsem, hsem,
                    layer):
    i = pl.program_id(0)
    more_tiles = i + 1 < pl.num_programs(0)

    def slot_of(k):
        return k % 2 if isinstance(k, int) else lax.rem(k, 2)

    def chunk_of(k):
        start = k * TF_MLP if isinstance(k, int) else pl.multiple_of(k * TF_MLP, TF_MLP)
        return pl.ds(start, TF_MLP)

    def w1_copy(k):
        s = slot_of(k)
        return pltpu.make_async_copy(w1_hbm.at[layer, :, chunk_of(k)], w1buf.at[s], wsem.at[0, s])

    def w2_copy(k):
        s = slot_of(k)
        return pltpu.make_async_copy(w2_hbm.at[layer, chunk_of(k), :], w2buf.at[s], wsem.at[1, s])

    def up(k):
        s = slot_of(k)
        mid = jnp.dot(xn_ref[...], w1buf[s], preferred_element_type=F32)
        mid = jnp.maximum(mid, 0.0)
        actbuf[s] = (mid * mid).astype(BF16)

    def down(k):
        s = slot_of(k)
        acc_ref[...] += jnp.dot(actbuf[s], w2buf[s], preferred_element_type=F32)

    rows = pl.ds(pl.multiple_of(i * TM_MLP, 16), TM_MLP)
    h_copy = pltpu.make_async_copy(h_hbm.at[rows, :], acc_ref, hsem.at[0])
    h_copy.start()

    @pl.when(i == 0)
    def _():
        w1_copy(0).start()
        w1_copy(1).start()
        w2_copy(0).start()

    w1_copy(0).wait()
    up(0)

    w1_copy(2).start()
    w2_copy(1).start()
    w1_copy(1).wait()
    up(1)
    h_copy.wait()
    w2_copy(0).wait()
    down(0)

    def step(f, carry):
        @pl.when(f + 2 < NF_MLP)
        def _():
            w1_copy(f + 2).start()

        @pl.when(jnp.logical_and(f + 2 == NF_MLP, more_tiles))
        def _():
            w1_copy(0).start()

        w2_copy(f + 1).start()
        w1_copy(f + 1).wait()
        w2_copy(f).wait()
        up(f + 1)
        down(f)
        return carry

    lax.fori_loop(1, NF_MLP - 1, step, 0)

    @pl.when(more_tiles)
    def _():
        w1_copy(1).start()
        w2_copy(0).start()

    w2_copy(NF_MLP - 1).wait()
    down(NF_MLP - 1)


def _mlp_kernel(h_hbm, xn_ref, w1_hbm, w2_hbm, g_ref, ho_ref, xo_ref, w1buf, w2buf, actbuf, wsem,
                hsem, *, layer):
    _mlp_accumulate(h_hbm, xn_ref, w1_hbm, w2_hbm, ho_ref, w1buf, w2buf, actbuf, wsem, hsem, layer)
    xo_ref[...] = _rms_gain(ho_ref[...], g_ref[...]).astype(xo_ref.dtype)


def _mlp_final_kernel(h_hbm, xn_ref, w1_hbm, w2_hbm, g_ref, out_hbm, w1buf, w2buf, actbuf, wsem,
                      hsem, ho_ref, obuf, osem, *, layer):
    _mlp_accumulate(h_hbm, xn_ref, w1_hbm, w2_hbm, ho_ref, w1buf, w2buf, actbuf, wsem, hsem, layer)
    i = pl.program_id(0)
    tiles_per_seq = SEQ_LEN // TM_MLP
    b = i // tiles_per_seq
    ti = i % tiles_per_seq
    body_rows = TM_MLP - N_META
    body = pltpu.make_async_copy(
        obuf.at[pl.ds(N_META, body_rows), :],
        out_hbm.at[b, pl.ds(pl.multiple_of(ti * TM_MLP, 16), body_rows), :], osem.at[0])
    head_start = pl.multiple_of(jnp.maximum(ti * TM_MLP - N_META, 0), 16)
    head = pltpu.make_async_copy(
        obuf.at[pl.ds(0, N_META), :], out_hbm.at[b, pl.ds(head_start, N_META), :], osem.at[1])

    @pl.when(i > 0)
    def _():
        body.wait()

    @pl.when(jnp.logical_and(i > 0, (i - 1) % tiles_per_seq > 0))
    def _():
        head.wait()

    obuf[...] = _rms_gain(ho_ref[...], g_ref[...])
    body.start()

    @pl.when(ti > 0)
    def _():
        head.start()

    @pl.when(i == pl.num_programs(0) - 1)
    def _():
        body.wait()
        head.wait()


def _mlp_call(h, xn, w1, w2, g, layer, final):
    row = lambda i: (i, 0)
    scratch = [pltpu.VMEM((2, D_MODEL, TF_MLP), BF16),
               pltpu.VMEM((2, TF_MLP, D_MODEL), BF16),
               pltpu.VMEM((2, TM_MLP, TF_MLP), BF16),
               pltpu.SemaphoreType.DMA((2, 2)),
               pltpu.SemaphoreType.DMA((1,))]
    if final:
        body = _mlp_final_kernel
        out_specs = pl.BlockSpec(memory_space=pl.ANY)
        out_shape = jax.ShapeDtypeStruct((BATCH, SEQ, D_MODEL), F32)
        scratch += [pltpu.VMEM((TM_MLP, D_MODEL), F32), pltpu.VMEM((TM_MLP, D_MODEL), F32),
                    pltpu.SemaphoreType.DMA((2,))]
    else:
        body = _mlp_kernel
        out_specs = [pl.BlockSpec((TM_MLP, D_MODEL), row), pl.BlockSpec((TM_MLP, D_MODEL), row)]
        out_shape = [jax.ShapeDtypeStruct((ROWS, D_MODEL), F32),
                     jax.ShapeDtypeStruct((ROWS, D_MODEL), BF16)]
    return pl.pallas_call(
        functools.partial(body, layer=layer),
        grid=(ROWS // TM_MLP,),
        in_specs=[pl.BlockSpec(memory_space=pl.ANY),
                  pl.BlockSpec((TM_MLP, D_MODEL), row),
                  pl.BlockSpec(memory_space=pl.ANY),
                  pl.BlockSpec(memory_space=pl.ANY),
                  pl.BlockSpec((1, D_MODEL), lambda i: (0, 0))],
        out_specs=out_specs,
        out_shape=out_shape,
        scratch_shapes=scratch,
        compiler_params=_params(("arbitrary",)),
        name="sq_relu_mlp_final" if final else "sq_relu_mlp",
    )(h, xn, w1, w2, g)


def _channel_dft_table():
    c = jnp.arange(GROUP_DIM, dtype=jnp.int32)
    ang = ((c[:, None] * c[None, :]) % GROUP_DIM).astype(F32) * (2.0 * math.pi / GROUP_DIM)
    return jnp.concatenate([jnp.cos(ang), jnp.sin(ang)], axis=1).astype(BF16)


def _sequence_dft_tables():
    scale = 2.0 * math.pi / DFT_LEN
    k = jnp.arange(DFT_K, dtype=jnp.int32)[None, :]
    mh = jnp.arange(DFT_LEN // TABLE_BLOCK, dtype=jnp.int32)[:, None] * TABLE_BLOCK
    ml = jnp.arange(TABLE_BLOCK, dtype=jnp.int32)[:, None]
    ang_h = ((mh * k) % DFT_LEN).astype(F32) * scale
    ang_l = ((ml * k) % DFT_LEN).astype(F32) * scale
    ch, sh = jnp.cos(ang_h)[:, None, :], jnp.sin(ang_h)[:, None, :]
    cl, sl = jnp.cos(ang_l)[None, :, :], jnp.sin(ang_l)[None, :, :]
    live = (k < DFT_LEN)[None]
    cm = jnp.where(live, ch * cl - sh * sl, 0.0).reshape(DFT_LEN, DFT_K).astype(BF16)
    sm = jnp.where(live, sh * cl + ch * sl, 0.0).reshape(DFT_LEN, DFT_K).astype(BF16)
    return cm, sm


def _twiddle_table():
    k = jnp.arange(DFT_K, dtype=jnp.int32)
    cols = []
    for r in range(1, RADIX):
        ang = (r * k).astype(F32) * (2.0 * math.pi / SEQ_LEN)
        cols += [jnp.cos(ang), jnp.sin(ang)]
    cols += [jnp.zeros_like(cols[0])] * (8 - len(cols))
    tw = jnp.stack(cols, axis=1)
    return jnp.where((k < DFT_LEN)[:, None], tw, 0.0)


def _interleave_permutation():
    j = jnp.arange(TM_OUT, dtype=jnp.int32)[:, None]
    c = jnp.arange(TM_OUT, dtype=jnp.int32)[None, :]
    src = (j % RADIX) * DFT_ROWS_PER_TILE + j // RADIX
    return (c == src).astype(BF16)


def kernel(x, meta_tokens, norm_mix_g, norm_mlp_g, norm_final_g, ab_w_in, ab_w_out, ab_conv_w,
           ab_conv_b, ab_ln_g, ab_ln_b, c_w_in, c_conv_w, c_w_out, mlp_w1, mlp_w2):
    depth = norm_mix_g.shape[0]
    meta = jnp.broadcast_to(meta_tokens[None].astype(x.dtype), (BATCH, N_META, D_MODEL))
    h = jnp.concatenate([meta, x], axis=1).reshape(ROWS, D_MODEL)

    cs = _channel_dft_table()
    cm, sm = _sequence_dft_tables()
    tw = _twiddle_table()
    perm = _interleave_permutation()

    ab_w_in, ab_w_out, c_w_in, c_w_out, mlp_w1, mlp_w2 = (
        w.astype(BF16) for w in (ab_w_in, ab_w_out, c_w_in, c_w_out, mlp_w1, mlp_w2))

    xn = _norm_call(h, norm_mix_g[0][None, :])
    for layer in range(depth):
        i = layer // 2
        g_mlp = norm_mlp_g[layer][None, :]
        if layer % 2 == 0:
            uc, us, glu = _even_in_call(xn, ab_w_in, cs, i)
            shape5 = (GROUPS, BATCH, RADIX, DFT_LEN, GROUP_DIM)
            p, q = _butterfly_call(uc.reshape(shape5), us.reshape(shape5), tw)
            ya = _dft_call(cm, sm, p, q)
            cw = jnp.broadcast_to(ab_conv_w[i][:, None, :], (CONV_KERNEL, SUBLANES, MIX_WIDTH))
            h, xn = _even_out_call(h, ya, glu, cw, ab_conv_b[i][None, :], ab_ln_g[i][None, :],
                                   ab_ln_b[i][None, :], perm, ab_w_out, g_mlp, i)
        else:
            gb, p = _odd_in_call(xn, c_w_in, i)
            h, xn = _odd_out_call(h, gb, p, c_conv_w[i], c_w_out, g_mlp, i)
        final = layer == depth - 1
        g_next = norm_final_g if final else norm_mix_g[layer + 1]
        if final:
            return _mlp_call(h, xn, mlp_w1, mlp_w2, g_next[None, :], layer, True)
        h, xn = _mlp_call(h, xn, mlp_w1, mlp_w2, g_next[None, :], layer, False)
```

```python
import functools
import math

import jax
import jax.numpy as jnp
from jax import lax
from jax.experimental import pallas as pl
from jax.experimental.pallas import tpu as pltpu

F32 = jnp.float32
BF16 = jnp.bfloat16

D_MODEL = 2048
BATCH = 2
SEQ = 8192
N_META = 16
SEQ_LEN = SEQ + N_META
ROWS = BATCH * SEQ_LEN
GROUPS = 4
GROUP_DIM = 256
MIX_WIDTH = GROUPS * GROUP_DIM
CONV_KERNEL = 31
CONV_PAD = (CONV_KERNEL - 1) // 2
D_FF = 4 * D_MODEL
EPS = 1e-6

RADIX = 9
DFT_LEN = SEQ_LEN // RADIX
DFT_K = 1024
DFT_NSPLIT = 3
TW_COLS = 2 * (RADIX - 1)

TM_IN = 912
TM_OUT = 432
TM_MLP = 912
TF_MLP = 1024
NF_MLP = D_FF // TF_MLP
ODD_CHUNK = 512
HALO = 16
SUBLANES = 8
CONV_RC = 48
SHIFT_ROWS = TM_OUT + 2 * HALO - SUBLANES
TILES_PER_SEQ = SEQ_LEN // TM_OUT
DFT_ROWS_PER_TILE = TM_OUT // RADIX
VMEM_LIMIT = 58 * 1024 * 1024

assert SEQ_LEN % TM_IN == 0 and SEQ_LEN % TM_OUT == 0 and ROWS % TM_MLP == 0
assert TM_OUT % CONV_RC == 0 and TM_OUT % HALO == 0 and CONV_RC % 16 == 0
assert DFT_LEN % 16 == 0 and DFT_ROWS_PER_TILE % 16 == 0 and TM_OUT % RADIX == 0
assert DFT_K % 256 == 0 and RADIX % DFT_NSPLIT == 0 and NF_MLP % 2 == 0


def _params(sem):
    return pltpu.CompilerParams(dimension_semantics=sem, vmem_limit_bytes=VMEM_LIMIT)


def _rms_gain(x, g):
    ms = jnp.mean(x * x, axis=-1, keepdims=True)
    return x * lax.rsqrt(ms + EPS) * g


def _resident(shape):
    nd = len(shape)
    return pl.BlockSpec(shape, lambda *_: (0,) * nd, pipeline_mode=pl.Buffered(1))


def _resident_layer(shape, layer):
    nd = len(shape)
    return pl.BlockSpec((None,) + shape, lambda *_: (layer,) + (0,) * nd,
                        pipeline_mode=pl.Buffered(1))


def _norm_kernel(h_ref, g_ref, o_ref):
    o_ref[...] = _rms_gain(h_ref[...], g_ref[...]).astype(o_ref.dtype)


def _norm_call(h, g):
    return pl.pallas_call(
        _norm_kernel,
        grid=(ROWS // TM_IN,),
        in_specs=[pl.BlockSpec((TM_IN, D_MODEL), lambda i: (i, 0)),
                  pl.BlockSpec((1, D_MODEL), lambda i: (0, 0))],
        out_specs=pl.BlockSpec((TM_IN, D_MODEL), lambda i: (i, 0)),
        out_shape=jax.ShapeDtypeStruct((ROWS, D_MODEL), BF16),
        compiler_params=_params(("parallel",)),
        name="prologue_norm",
    )(h, g)


def _even_in_kernel(xn_ref, wu_ref, wa_ref, wg_ref, cs_ref, uc_ref, us_ref, glu_ref, wcat):
    wcat[:, 0:GROUP_DIM] = wu_ref[...]
    wcat[:, GROUP_DIM:2 * GROUP_DIM] = wa_ref[...]
    wcat[:, 2 * GROUP_DIM:] = wg_ref[...]
    proj = jnp.dot(xn_ref[...], wcat[...], preferred_element_type=F32)
    u = proj[:, :GROUP_DIM].astype(BF16)
    cs = jnp.dot(u, cs_ref[...], preferred_element_type=F32)
    uc_ref[...] = cs[:, :GROUP_DIM].astype(BF16)
    us_ref[...] = cs[:, GROUP_DIM:].astype(BF16)
    a = proj[:, GROUP_DIM:2 * GROUP_DIM]
    gate = proj[:, 2 * GROUP_DIM:]
    glu_ref[...] = a * jax.nn.sigmoid(gate)


def _even_in_call(xn, w, cs, layer):
    grp = jax.ShapeDtypeStruct((GROUPS, ROWS, GROUP_DIM), BF16)
    wspec = lambda part: pl.BlockSpec((None, D_MODEL, GROUP_DIM),
                                      lambda i, g: (layer, 0, part * GROUPS + g))
    return pl.pallas_call(
        _even_in_kernel,
        grid=(ROWS // TM_IN, GROUPS),
        in_specs=[pl.BlockSpec((TM_IN, D_MODEL), lambda i, g: (i, 0)),
                  wspec(0), wspec(1), wspec(2),
                  pl.BlockSpec((GROUP_DIM, 2 * GROUP_DIM), lambda i, g: (0, 0))],
        out_specs=[pl.BlockSpec((None, TM_IN, GROUP_DIM), lambda i, g: (g, i, 0)),
                   pl.BlockSpec((None, TM_IN, GROUP_DIM), lambda i, g: (g, i, 0)),
                   pl.BlockSpec((TM_IN, GROUP_DIM), lambda i, g: (i, g))],
        out_shape=[grp, grp, jax.ShapeDtypeStruct((ROWS, MIX_WIDTH), F32)],
        scratch_shapes=[pltpu.VMEM((D_MODEL, 3 * GROUP_DIM), BF16)],
        compiler_params=_params(("parallel", "arbitrary")),
        name="even_in_proj",
    )(xn, w, w, w, cs)


def _dft3(z0, z1, z2):
    half_sqrt3 = 0.5 * math.sqrt(3.0)
    sx, sy = z1[0] + z2[0], z1[1] + z2[1]
    dx, dy = z1[0] - z2[0], z1[1] - z2[1]
    mx, my = z0[0] - 0.5 * sx, z0[1] - 0.5 * sy
    return ((z0[0] + sx, z0[1] + sy),
            (mx + half_sqrt3 * dy, my - half_sqrt3 * dx),
            (mx - half_sqrt3 * dy, my + half_sqrt3 * dx))


def _butterfly_kernel(uc_ref, us_ref, tw_ref, p_ref, q_ref):
    lane_blocks = GROUP_DIM // 128

    def chunk(c, carry):
        r0 = pl.multiple_of(c * 16, 16)
        rows = pl.ds(r0, 16)
        tw = tw_ref[rows, :]
        row_id = r0 + lax.broadcasted_iota(jnp.int32, (16, 1), 0)
        valid = row_id < DFT_LEN
        for lb in range(lane_blocks):
            lanes = slice(lb * 128, (lb + 1) * 128)
            z = [(uc_ref[k, rows, lanes].astype(F32), -us_ref[k, rows, lanes].astype(F32))
                 for k in range(RADIX)]
            inner = []
            for q0 in range(3):
                outs = _dft3(z[q0], z[q0 + 3], z[q0 + 6])
                row = []
                for r0_ in range(3):
                    x, y = outs[r0_]
                    if r0_ * q0:
                        ang = -2.0 * math.pi * r0_ * q0 / 9.0
                        ct, st = math.cos(ang), math.sin(ang)
                        x, y = ct * x - st * y, ct * y + st * x
                    row.append((x, y))
                inner.append(row)
            for r0_ in range(3):
                outs = _dft3(inner[0][r0_], inner[1][r0_], inner[2][r0_])
                for r1 in range(3):
                    r = r0_ + 3 * r1
                    x, y = outs[r1]
                    if r == 0:
                        p, q = x, y
                    else:
                        cr = tw[:, 2 * r - 2:2 * r - 1]
                        sr = tw[:, 2 * r - 1:2 * r]
                        p = cr * x + sr * y
                        q = cr * y - sr * x
                    out_lanes = slice(r * GROUP_DIM + lb * 128, r * GROUP_DIM + (lb + 1) * 128)
                    p_ref[rows, out_lanes] = jnp.where(valid, p, 0.0).astype(BF16)
                    q_ref[rows, out_lanes] = jnp.where(valid, q, 0.0).astype(BF16)
        return carry

    lax.fori_loop(0, DFT_K // 16, chunk, 0, unroll=2)


def _butterfly_call(uc, us, tw):
    in_spec = pl.BlockSpec((None, None, RADIX, DFT_K, GROUP_DIM), lambda g, b: (g, b, 0, 0, 0))
    out_spec = pl.BlockSpec((None, None, DFT_K, RADIX * GROUP_DIM), lambda g, b: (g, b, 0, 0))
    out = jax.ShapeDtypeStruct((GROUPS, BATCH, DFT_K, RADIX * GROUP_DIM), BF16)
    return pl.pallas_call(
        _butterfly_kernel,
        grid=(GROUPS, BATCH),
        in_specs=[in_spec, in_spec, pl.BlockSpec((DFT_K, TW_COLS), lambda g, b: (0, 0))],
        out_specs=[out_spec, out_spec],
        out_shape=[out, out],
        compiler_params=_params(("parallel", "parallel")),
        name="dft_butterfly",
    )(uc, us, tw)


def _dft_kernel(c_ref, s_ref, p_ref, q_ref, o_ref, *, scale):
    acc = jnp.dot(c_ref[...], p_ref[...], preferred_element_type=F32)
    acc = acc + jnp.dot(s_ref[...], q_ref[...], preferred_element_type=F32)
    for r in range(DFT_NSPLIT):
        o_ref[r] = (acc[:, r * GROUP_DIM:(r + 1) * GROUP_DIM] * scale).astype(o_ref.dtype)


def _dft_call(cm, sm, p, q):
    width = DFT_NSPLIT * GROUP_DIM
    in_spec = pl.BlockSpec((None, None, DFT_K, width), lambda g, b, j: (g, b, 0, j))
    return pl.pallas_call(
        functools.partial(_dft_kernel, scale=1.0 / math.sqrt(SEQ_LEN * GROUP_DIM)),
        grid=(GROUPS, BATCH, RADIX // DFT_NSPLIT),
        in_specs=[_resident((DFT_LEN, DFT_K)), _resident((DFT_LEN, DFT_K)), in_spec, in_spec],
        out_specs=pl.BlockSpec((None, None, DFT_NSPLIT, DFT_LEN, GROUP_DIM),
                               lambda g, b, j: (g, b, j, 0, 0)),
        out_shape=jax.ShapeDtypeStruct((GROUPS, BATCH, RADIX, DFT_LEN, GROUP_DIM), BF16),
        compiler_params=_params(("parallel", "parallel", "parallel")),
        name="dft_matmul",
    )(cm, sm, p, q)


def _even_out_kernel(h_ref, ya_ref, xm_ref, xp_ref, xnx_ref, cw_ref, cb_ref, lg_ref, lb_ref,
                     perm_ref, w_ref, g_ref, ho_ref, xo_ref, xpad, xshift, yperm, ycat):
    i = pl.program_id(0)
    first = (i % TILES_PER_SEQ) == 0
    last = (i % TILES_PER_SEQ) == TILES_PER_SEQ - 1
    xpad[0:HALO, :] = jnp.where(first, 0.0, xp_ref[...])
    xpad[HALO:HALO + TM_OUT, :] = xm_ref[...]
    xpad[HALO + TM_OUT:, :] = jnp.where(last, 0.0, xnx_ref[...])

    for g in range(GROUPS):
        for r in range(RADIX):
            yperm[r * DFT_ROWS_PER_TILE:(r + 1) * DFT_ROWS_PER_TILE,
                  g * GROUP_DIM:(g + 1) * GROUP_DIM] = ya_ref[g, r]
    ycat[:, :MIX_WIDTH] = jnp.dot(perm_ref[...], yperm[...],
                                  preferred_element_type=F32).astype(BF16)
    ho_ref[...] = h_ref[...] + jnp.dot(ycat[:, :MIX_WIDTH], w_ref[:MIX_WIDTH, :],
                                       preferred_element_type=F32)

    row_vregs = CONV_RC // SUBLANES
    for g in range(GROUPS):
        lanes = slice(g * GROUP_DIM, (g + 1) * GROUP_DIM)
        out_lanes = slice(MIX_WIDTH + g * GROUP_DIM, MIX_WIDTH + (g + 1) * GROUP_DIM)
        for s in range(SUBLANES):
            xshift[g, s] = xpad[s:s + SHIFT_ROWS, lanes]
        for c in range(TM_OUT // CONV_RC):
            r0 = c * CONV_RC
            acc = [jnp.zeros((SUBLANES, GROUP_DIM), F32) for _ in range(row_vregs)]
            for j in range(CONV_KERNEL):
                oq, os_ = divmod(HALO - CONV_PAD + j, SUBLANES)
                w = cw_ref[j, :, lanes]
                for r in range(row_vregs):
                    start = r0 + SUBLANES * (oq + r)
                    acc[r] = acc[r] + w * xshift[g, os_, start:start + SUBLANES, :]
            conv = jnp.concatenate(acc, axis=0) + cb_ref[:, lanes]
            mu = jnp.mean(conv, axis=-1, keepdims=True)
            dev = conv - mu
            var = jnp.mean(dev * dev, axis=-1, keepdims=True)
            y = dev * lax.rsqrt(var + EPS) * lg_ref[:, lanes] + lb_ref[:, lanes]
            y = y * jax.nn.sigmoid(y)
            ycat[r0:r0 + CONV_RC, out_lanes] = y.astype(BF16)
        ho_ref[...] += jnp.dot(ycat[:, out_lanes], w_ref[out_lanes, :],
                               preferred_element_type=F32)

    xo_ref[...] = _rms_gain(ho_ref[...], g_ref[...]).astype(xo_ref.dtype)


def _even_out_call(h, ya, glu, cw, cb, lg, lb, perm, w, g, layer):
    hb = TM_OUT // HALO
    n_halo = ROWS // HALO
    row = lambda i: (i, 0)
    const = lambda i: (0, 0)
    return pl.pallas_call(
        _even_out_kernel,
        grid=(ROWS // TM_OUT,),
        in_specs=[pl.BlockSpec((TM_OUT, D_MODEL), row),
                  pl.BlockSpec((GROUPS, None, RADIX, DFT_ROWS_PER_TILE, GROUP_DIM),
                               lambda i: (0, i // TILES_PER_SEQ, 0, i % TILES_PER_SEQ, 0)),
                  pl.BlockSpec((TM_OUT, MIX_WIDTH), row),
                  pl.BlockSpec((HALO, MIX_WIDTH), lambda i: (jnp.maximum(i * hb - 1, 0), 0)),
                  pl.BlockSpec((HALO, MIX_WIDTH),
                               lambda i: (jnp.minimum((i + 1) * hb, n_halo - 1), 0)),
                  pl.BlockSpec((CONV_KERNEL, SUBLANES, MIX_WIDTH), lambda i: (0, 0, 0)),
                  pl.BlockSpec((1, MIX_WIDTH), const),
                  pl.BlockSpec((1, MIX_WIDTH), const),
                  pl.BlockSpec((1, MIX_WIDTH), const),
                  pl.BlockSpec((TM_OUT, TM_OUT), const),
                  _resident_layer((D_MODEL, D_MODEL), layer),
                  pl.BlockSpec((1, D_MODEL), const)],
        out_specs=[pl.BlockSpec((TM_OUT, D_MODEL), row), pl.BlockSpec((TM_OUT, D_MODEL), row)],
        out_shape=[jax.ShapeDtypeStruct((ROWS, D_MODEL), F32),
                   jax.ShapeDtypeStruct((ROWS, D_MODEL), BF16)],
        scratch_shapes=[pltpu.VMEM((TM_OUT + 2 * HALO, MIX_WIDTH), F32),
                        pltpu.VMEM((GROUPS, SUBLANES, SHIFT_ROWS, GROUP_DIM), F32),
                        pltpu.VMEM((TM_OUT, MIX_WIDTH), BF16),
                        pltpu.VMEM((TM_OUT, D_MODEL), BF16)],
        compiler_params=_params(("parallel",)),
        name="even_out_proj",
    )(h, ya, glu, glu, glu, cw, cb, lg, lb, perm, w, g)


def _odd_in_kernel(xn_ref, wb_ref, wc_ref, wh_ref, gb_ref, p_ref):
    xn = xn_ref[...]
    gb_ref[...] = jnp.dot(xn, wb_ref[...], preferred_element_type=F32).astype(gb_ref.dtype)
    gc = jnp.dot(xn, wc_ref[...], preferred_element_type=F32)
    hin = jnp.dot(xn, wh_ref[...], preferred_element_type=F32)
    p_ref[...] = (gc * hin).astype(p_ref.dtype)


def _odd_in_call(xn, w, layer):
    n_chunks = D_MODEL // ODD_CHUNK
    out = jax.ShapeDtypeStruct((ROWS, D_MODEL), BF16)
    wspec = lambda part: pl.BlockSpec((None, D_MODEL, ODD_CHUNK),
                                      lambda i, n: (layer, 0, part * n_chunks + n))
    return pl.pallas_call(
        _odd_in_kernel,
        grid=(ROWS // TM_IN, n_chunks),
        in_specs=[pl.BlockSpec((TM_IN, D_MODEL), lambda i, n: (i, 0)),
                  wspec(0), wspec(1), wspec(2)],
        out_specs=[pl.BlockSpec((TM_IN, ODD_CHUNK), lambda i, n: (i, n)),
                   pl.BlockSpec((TM_IN, ODD_CHUNK), lambda i, n: (i, n))],
        out_shape=[out, out],
        compiler_params=_params(("parallel", "arbitrary")),
        name="odd_in_proj",
    )(xn, w, w, w)


def _odd_out_kernel(h_ref, gb_ref, pm_ref, pp_ref, pn_ref, cw_ref, w_ref, g_ref,
                    ho_ref, xo_ref, xmid, xprev, xnext, ybuf):
    i = pl.program_id(0)
    first = (i % TILES_PER_SEQ) == 0
    last = (i % TILES_PER_SEQ) == TILES_PER_SEQ - 1
    xmid[...] = pm_ref[...].astype(F32)
    xprev[0:1, :] = jnp.where(first, 0.0, pp_ref[HALO - 1:HALO, :].astype(F32))
    xprev[1:TM_OUT, :] = xmid[0:TM_OUT - 1, :]
    xnext[0:TM_OUT - 1, :] = xmid[1:TM_OUT, :]
    xnext[TM_OUT - 1:TM_OUT, :] = jnp.where(last, 0.0, pn_ref[0:1, :].astype(F32))

    n_blocks = 4
    width = D_MODEL // n_blocks
    for blk in range(n_blocks):
        lanes = slice(blk * width, (blk + 1) * width)
        for c in range(TM_OUT // 16):
            rows = slice(c * 16, (c + 1) * 16)
            conv = cw_ref[0:1, lanes] * xprev[rows, lanes]
            conv = conv + cw_ref[1:2, lanes] * xmid[rows, lanes]
            conv = conv + cw_ref[2:3, lanes] * xnext[rows, lanes]
            ybuf[rows, lanes] = (gb_ref[rows, lanes].astype(F32) * conv).astype(BF16)
        part = jnp.dot(ybuf[:, lanes], w_ref[lanes, :], preferred_element_type=F32)
        if blk == 0:
            ho_ref[...] = h_ref[...] + part
        else:
            ho_ref[...] += part

    xo_ref[...] = _rms_gain(ho_ref[...], g_ref[...]).astype(xo_ref.dtype)


def _odd_out_call(h, gb, p, cw, w, g, layer):
    hb = TM_OUT // HALO
    n_halo = ROWS // HALO
    row = lambda i: (i, 0)
    const = lambda i: (0, 0)
    return pl.pallas_call(
        _odd_out_kernel,
        grid=(ROWS // TM_OUT,),
        in_specs=[pl.BlockSpec((TM_OUT, D_MODEL), row),
                  pl.BlockSpec((TM_OUT, D_MODEL), row),
                  pl.BlockSpec((TM_OUT, D_MODEL), row),
                  pl.BlockSpec((HALO, D_MODEL), lambda i: (jnp.maximum(i * hb - 1, 0), 0)),
                  pl.BlockSpec((HALO, D_MODEL),
                               lambda i: (jnp.minimum((i + 1) * hb, n_halo - 1), 0)),
                  pl.BlockSpec((3, D_MODEL), const),
                  _resident_layer((D_MODEL, D_MODEL), layer),
                  pl.BlockSpec((1, D_MODEL), const)],
        out_specs=[pl.BlockSpec((TM_OUT, D_MODEL), row), pl.BlockSpec((TM_OUT, D_MODEL), row)],
        out_shape=[jax.ShapeDtypeStruct((ROWS, D_MODEL), F32),
                   jax.ShapeDtypeStruct((ROWS, D_MODEL), BF16)],
        scratch_shapes=[pltpu.VMEM((TM_OUT, D_MODEL), F32),
                        pltpu.VMEM((TM_OUT, D_MODEL), F32),
                        pltpu.VMEM((TM_OUT, D_MODEL), F32),
                        pltpu.VMEM((TM_OUT, D_MODEL), BF16)],
        compiler_params=_params(("parallel",)),
        name="odd_out_proj",
    )(h, gb, p, p, p, cw, w, g)


def _mlp_accumulate(h_hbm, xn_ref, w1_hbm, w2_hbm, acc_ref, w1buf, w2buf, actbuf, wsem, hsem,
                    layer):
    i = pl.program_id(0)
    more_tiles = i + 1 < pl.num_programs(0)

    def slot_of(k):
        return k % 2 if isinstance(k, int) else lax.rem(k, 2)

    def chunk_of(k):
        start = k * TF_MLP if isinstance(k, int) else pl.multiple_of(k * TF_MLP, TF_MLP)
        return pl.ds(start, TF_MLP)

    def w1_copy(k):
        s = slot_of(k)
        return pltpu.make_async_copy(w1_hbm.at[layer, :, chunk_of(k)], w1buf.at[s], wsem.at[0, s])

    def w2_copy(k):
        s = slot_of(k)
        return pltpu.make_async_copy(w2_hbm.at[layer, chunk_of(k), :], w2buf.at[s], wsem.at[1, s])

    def up(k):
        s = slot_of(k)
        mid = jnp.dot(xn_ref[...], w1buf[s], preferred_element_type=F32)
        mid = jnp.maximum(mid, 0.0)
        actbuf[s] = (mid * mid).astype(BF16)

    def down(k):
        s = slot_of(k)
        acc_ref[...] += jnp.dot(actbuf[s], w2buf[s], preferred_element_type=F32)

    rows = pl.ds(pl.multiple_of(i * TM_MLP, 16), TM_MLP)
    h_copy = pltpu.make_async_copy(h_hbm.at[rows, :], acc_ref, hsem.at[0])
    h_copy.start()

    @pl.when(i == 0)
    def _():
        w1_copy(0).start()
        w1_copy(1).start()
        w2_copy(0).start()

    w1_copy(0).wait()
    up(0)

    w1_copy(2).start()
    w2_copy(1).start()
    w1_copy(1).wait()
    up(1)
    h_copy.wait()
    w2_copy(0).wait()
    down(0)

    def step(f, carry):
        @pl.when(f + 2 < NF_MLP)
        def _():
            w1_copy(f + 2).start()

        @pl.when(jnp.logical_and(f + 2 == NF_MLP, more_tiles))
        def _():
            w1_copy(0).start()

        w2_copy(f + 1).start()
        w1_copy(f + 1).wait()
        w2_copy(f).wait()
        up(f + 1)
        down(f)
        return carry

    lax.fori_loop(1, NF_MLP - 1, step, 0)

    @pl.when(more_tiles)
    def _():
        w1_copy(1).start()
        w2_copy(0).start()

    w2_copy(NF_MLP - 1).wait()
    down(NF_MLP - 1)


def _mlp_kernel(h_hbm, xn_ref, w1_hbm, w2_hbm, g_ref, ho_ref, xo_ref, w1buf, w2buf, actbuf, wsem,
                hsem, *, layer):
    _mlp_accumulate(h_hbm, xn_ref, w1_hbm, w2_hbm, ho_ref, w1buf, w2buf, actbuf, wsem, hsem, layer)
    xo_ref[...] = _rms_gain(ho_ref[...], g_ref[...]).astype(xo_ref.dtype)


def _mlp_final_kernel(h_hbm, xn_ref, w1_hbm, w2_hbm, g_ref, out_hbm, w1buf, w2buf, actbuf, wsem,
                      hsem, ho_ref, obuf, osem, *, layer):
    _mlp_accumulate(h_hbm, xn_ref, w1_hbm, w2_hbm, ho_ref, w1buf, w2buf, actbuf, wsem, hsem, layer)
    i = pl.program_id(0)
    tiles_per_seq = SEQ_LEN // TM_MLP
    b = i // tiles_per_seq
    ti = i % tiles_per_seq
    body_rows = TM_MLP - N_META
    body = pltpu.make_async_copy(
        obuf.at[pl.ds(N_META, body_rows), :],
        out_hbm.at[b, pl.ds(pl.multiple_of(ti * TM_MLP, 16), body_rows), :], osem.at[0])
    head_start = pl.multiple_of(jnp.maximum(ti * TM_MLP - N_META, 0), 16)
    head = pltpu.make_async_copy(
        obuf.at[pl.ds(0, N_META), :], out_hbm.at[b, pl.ds(head_start, N_META), :], osem.at[1])

    @pl.when(i > 0)
    def _():
        body.wait()

    @pl.when(jnp.logical_and(i > 0, (i - 1) % tiles_per_seq > 0))
    def _():
        head.wait()

    obuf[...] = _rms_gain(ho_ref[...], g_ref[...])
    body.start()

    @pl.when(ti > 0)
    def _():
        head.start()

    @pl.when(i == pl.num_programs(0) - 1)
    def _():
        body.wait()
        head.wait()


def _mlp_call(h, xn, w1, w2, g, layer, final):
    row = lambda i: (i, 0)
    scratch = [pltpu.VMEM((2, D_MODEL, TF_MLP), BF16),
               pltpu.VMEM((2, TF_MLP, D_MODEL), BF16),
               pltpu.VMEM((2, TM_MLP, TF_MLP), BF16),
               pltpu.SemaphoreType.DMA((2, 2)),
               pltpu.SemaphoreType.DMA((1,))]
    if final:
        body = _mlp_final_kernel
        out_specs = pl.BlockSpec(memory_space=pl.ANY)
        out_shape = jax.ShapeDtypeStruct((BATCH, SEQ, D_MODEL), F32)
        scratch += [pltpu.VMEM((TM_MLP, D_MODEL), F32), pltpu.VMEM((TM_MLP, D_MODEL), F32),
                    pltpu.SemaphoreType.DMA((2,))]
    else:
        body = _mlp_kernel
        out_specs = [pl.BlockSpec((TM_MLP, D_MODEL), row), pl.BlockSpec((TM_MLP, D_MODEL), row)]
        out_shape = [jax.ShapeDtypeStruct((ROWS, D_MODEL), F32),
                     jax.ShapeDtypeStruct((ROWS, D_MODEL), BF16)]
    return pl.pallas_call(
        functools.partial(body, layer=layer),
        grid=(ROWS // TM_MLP,),
        in_specs=[pl.BlockSpec(memory_space=pl.ANY),
                  pl.BlockSpec((TM_MLP, D_MODEL), row),
                  pl.BlockSpec(memory_space=pl.ANY),
                  pl.BlockSpec(memory_space=pl.ANY),
                  pl.BlockSpec((1, D_MODEL), lambda i: (0, 0))],
        out_specs=out_specs,
        out_shape=out_shape,
        scratch_shapes=scratch,
        compiler_params=_params(("arbitrary",)),
        name="sq_relu_mlp_final" if final else "sq_relu_mlp",
    )(h, xn, w1, w2, g)


def _channel_dft_table():
    c = jnp.arange(GROUP_DIM, dtype=jnp.int32)
    ang = ((c[:, None] * c[None, :]) % GROUP_DIM).astype(F32) * (2.0 * math.pi / GROUP_DIM)
    return jnp.concatenate([jnp.cos(ang), jnp.sin(ang)], axis=1).astype(BF16)


def _sequence_dft_tables():
    m = jnp.arange(DFT_LEN, dtype=jnp.int32)[:, None]
    k = jnp.arange(DFT_K, dtype=jnp.int32)[None, :]
    ang = ((m * k) % DFT_LEN).astype(F32) * (2.0 * math.pi / DFT_LEN)
    live = k < DFT_LEN
    cm = jnp.where(live, jnp.cos(ang), 0.0).astype(BF16)
    sm = jnp.where(live, jnp.sin(ang), 0.0).astype(BF16)
    return cm, sm


def _twiddle_table():
    k = jnp.arange(DFT_K, dtype=jnp.int32)
    cols = []
    for r in range(1, RADIX):
        ang = (r * k).astype(F32) * (2.0 * math.pi / SEQ_LEN)
        cols += [jnp.cos(ang), jnp.sin(ang)]
    tw = jnp.stack(cols, axis=1)
    return jnp.where((k < DFT_LEN)[:, None], tw, 0.0)


def _interleave_permutation():
    j = jnp.arange(TM_OUT, dtype=jnp.int32)[:, None]
    c = jnp.arange(TM_OUT, dtype=jnp.int32)[None, :]
    src = (j % RADIX) * DFT_ROWS_PER_TILE + j // RADIX
    return (c == src).astype(BF16)


def kernel(x, meta_tokens, norm_mix_g, norm_mlp_g, norm_final_g, ab_w_in, ab_w_out, ab_conv_w,
           ab_conv_b, ab_ln_g, ab_ln_b, c_w_in, c_conv_w, c_w_out, mlp_w1, mlp_w2):
    depth = norm_mix_g.shape[0]
    meta = jnp.broadcast_to(meta_tokens[None].astype(x.dtype), (BATCH, N_META, D_MODEL))
    h = jnp.concatenate([meta, x], axis=1).reshape(ROWS, D_MODEL)

    cs = _channel_dft_table()
    cm, sm = _sequence_dft_tables()
    tw = _twiddle_table()
    perm = _interleave_permutation()

    ab_w_in, ab_w_out, c_w_in, c_w_out, mlp_w1, mlp_w2 = (
        w.astype(BF16) for w in (ab_w_in, ab_w_out, c_w_in, c_w_out, mlp_w1, mlp_w2))

    xn = _norm_call(h, norm_mix_g[0][None, :])
    for layer in range(depth):
        i = layer // 2
        g_mlp = norm_mlp_g[layer][None, :]
        if layer % 2 == 0:
            uc, us, glu = _even_in_call(xn, ab_w_in, cs, i)
            shape5 = (GROUPS, BATCH, RADIX, DFT_LEN, GROUP_DIM)
            p, q = _butterfly_call(uc.reshape(shape5), us.reshape(shape5), tw)
            ya = _dft_call(cm, sm, p, q)
            cw = jnp.broadcast_to(ab_conv_w[i][:, None, :], (CONV_KERNEL, SUBLANES, MIX_WIDTH))
            h, xn = _even_out_call(h, ya, glu, cw, ab_conv_b[i][None, :], ab_ln_g[i][None, :],
                                   ab_ln_b[i][None, :], perm, ab_w_out, g_mlp, i)
        else:
            gb, p = _odd_in_call(xn, c_w_in, i)
            h, xn = _odd_out_call(h, gb, p, c_conv_w[i], c_w_out, g_mlp, i)
        final = layer == depth - 1
        g_next = norm_final_g if final else norm_mix_g[layer + 1]
        if final:
            return _mlp_call(h, xn, mlp_w1, mlp_w2, g_next[None, :], layer, True)
        h, xn = _mlp_call(h, xn, mlp_w1, mlp_w2, g_next[None, :], layer, False)
```

```python
import functools
import math

import jax
import jax.numpy as jnp
from jax import lax
from jax.experimental import pallas as pl
from jax.experimental.pallas import tpu as pltpu

F32 = jnp.float32
BF16 = jnp.bfloat16

D_MODEL = 2048
BATCH = 2
SEQ = 8192
N_META = 16
SEQ_LEN = SEQ + N_META
ROWS = BATCH * SEQ_LEN
GROUPS = 4
GROUP_DIM = 256
MIX_WIDTH = GROUPS * GROUP_DIM
CONV_KERNEL = 31
CONV_PAD = (CONV_KERNEL - 1) // 2
D_FF = 4 * D_MODEL
EPS = 1e-6

RADIX = 9
DFT_LEN = SEQ_LEN // RADIX
DFT_K = 1024
DFT_NSPLIT = 3
TW_COLS = 2 * (RADIX - 1)

TM_IN = 912
TM_OUT = 432
TM_MLP = 912
TF_MLP = 1024
NF_MLP = D_FF // TF_MLP
MLP_FINISH_BLOCKS = (slice(0, 448), slice(448, TM_MLP))
ODD_CHUNK = 512
HALO = 16
SUBLANES = 8
CONV_RC = 48
SHIFT_ROWS = TM_OUT + 2 * HALO - SUBLANES
TILES_PER_SEQ = SEQ_LEN // TM_OUT
DFT_ROWS_PER_TILE = TM_OUT // RADIX
VMEM_LIMIT = 58 * 1024 * 1024

assert SEQ_LEN % TM_IN == 0 and SEQ_LEN % TM_OUT == 0 and ROWS % TM_MLP == 0
assert TM_OUT % CONV_RC == 0 and TM_OUT % HALO == 0 and CONV_RC % 16 == 0
assert DFT_LEN % 16 == 0 and DFT_ROWS_PER_TILE % 16 == 0 and TM_OUT % RADIX == 0
assert DFT_K % 256 == 0 and RADIX % DFT_NSPLIT == 0 and NF_MLP % 2 == 0


def _params(sem):
    return pltpu.CompilerParams(dimension_semantics=sem, vmem_limit_bytes=VMEM_LIMIT)


def _rms_gain(x, g):
    ms = jnp.mean(x * x, axis=-1, keepdims=True)
    return x * lax.rsqrt(ms + EPS) * g


def _resident(shape):
    nd = len(shape)
    return pl.BlockSpec(shape, lambda *_: (0,) * nd, pipeline_mode=pl.Buffered(1))


def _resident_layer(shape, layer):
    nd = len(shape)
    return pl.BlockSpec((None,) + shape, lambda *_: (layer,) + (0,) * nd,
                        pipeline_mode=pl.Buffered(1))


def _prologue_kernel(x_hbm, meta_ref, g_ref, h_ref, xn_ref, xbuf, sem):
    i = pl.program_id(0)
    tiles_per_seq = SEQ_LEN // TM_IN
    body_rows = TM_IN - N_META

    def copies(t, slot):
        b = t // tiles_per_seq
        ti = t % tiles_per_seq
        body = pltpu.make_async_copy(
            x_hbm.at[b, pl.ds(pl.multiple_of(ti * TM_IN, 16), body_rows), :],
            xbuf.at[slot, pl.ds(N_META, body_rows), :], sem.at[0, slot])
        head_start = pl.multiple_of(jnp.maximum(ti * TM_IN - N_META, 0), 16)
        head = pltpu.make_async_copy(
            x_hbm.at[b, pl.ds(head_start, N_META), :],
            xbuf.at[slot, pl.ds(0, N_META), :], sem.at[1, slot])
        return body, head, ti

    def start(t, slot):
        body, head, ti = copies(t, slot)
        body.start()

        @pl.when(ti > 0)
        def _():
            head.start()

    slot = lax.rem(i, 2)

    @pl.when(i == 0)
    def _():
        start(0, 0)

    @pl.when(i + 1 < pl.num_programs(0))
    def _():
        start(i + 1, 1 - slot)

    body, head, ti = copies(i, slot)
    body.wait()

    @pl.when(ti > 0)
    def _():
        head.wait()

    @pl.when(ti == 0)
    def _():
        xbuf[slot, 0:N_META, :] = meta_ref[...]

    hval = xbuf[slot]
    h_ref[...] = hval
    xn_ref[...] = _rms_gain(hval, g_ref[...]).astype(xn_ref.dtype)


def _prologue_call(x, meta, g):
    row = lambda i: (i, 0)
    return pl.pallas_call(
        _prologue_kernel,
        grid=(ROWS // TM_IN,),
        in_specs=[pl.BlockSpec(memory_space=pl.ANY),
                  pl.BlockSpec((N_META, D_MODEL), lambda i: (0, 0)),
                  pl.BlockSpec((1, D_MODEL), lambda i: (0, 0))],
        out_specs=[pl.BlockSpec((TM_IN, D_MODEL), row), pl.BlockSpec((TM_IN, D_MODEL), row)],
        out_shape=[jax.ShapeDtypeStruct((ROWS, D_MODEL), F32),
                   jax.ShapeDtypeStruct((ROWS, D_MODEL), BF16)],
        scratch_shapes=[pltpu.VMEM((2, TM_IN, D_MODEL), F32), pltpu.SemaphoreType.DMA((2, 2))],
        compiler_params=_params(("arbitrary",)),
        name="prologue_concat_norm",
    )(x, meta, g)


def _even_in_kernel(xn_ref, wu_ref, wa_ref, wg_ref, cs_ref, uc_ref, us_ref, glu_ref, wcat):
    wcat[:, 0:GROUP_DIM] = wu_ref[...]
    wcat[:, GROUP_DIM:2 * GROUP_DIM] = wa_ref[...]
    wcat[:, 2 * GROUP_DIM:] = wg_ref[...]
    proj = jnp.dot(xn_ref[...], wcat[...], preferred_element_type=F32)
    u = proj[:, :GROUP_DIM].astype(BF16)
    cs = jnp.dot(u, cs_ref[...], preferred_element_type=F32)
    uc_ref[...] = cs[:, :GROUP_DIM].astype(BF16)
    us_ref[...] = cs[:, GROUP_DIM:].astype(BF16)
    a = proj[:, GROUP_DIM:2 * GROUP_DIM]
    gate = proj[:, 2 * GROUP_DIM:]
    glu_ref[...] = a * jax.nn.sigmoid(gate)


def _even_in_call(xn, w, cs, layer):
    grp = jax.ShapeDtypeStruct((GROUPS, ROWS, GROUP_DIM), BF16)
    wspec = lambda part: pl.BlockSpec((None, D_MODEL, GROUP_DIM),
                                      lambda i, g: (layer, 0, part * GROUPS + g))
    return pl.pallas_call(
        _even_in_kernel,
        grid=(ROWS // TM_IN, GROUPS),
        in_specs=[pl.BlockSpec((TM_IN, D_MODEL), lambda i, g: (i, 0)),
                  wspec(0), wspec(1), wspec(2),
                  pl.BlockSpec((GROUP_DIM, 2 * GROUP_DIM), lambda i, g: (0, 0))],
        out_specs=[pl.BlockSpec((None, TM_IN, GROUP_DIM), lambda i, g: (g, i, 0)),
                   pl.BlockSpec((None, TM_IN, GROUP_DIM), lambda i, g: (g, i, 0)),
                   pl.BlockSpec((TM_IN, GROUP_DIM), lambda i, g: (i, g))],
        out_shape=[grp, grp, jax.ShapeDtypeStruct((ROWS, MIX_WIDTH), F32)],
        scratch_shapes=[pltpu.VMEM((D_MODEL, 3 * GROUP_DIM), BF16)],
        compiler_params=_params(("parallel", "arbitrary")),
        name="even_in_proj",
    )(xn, w, w, w, cs)


def _dft3(z0, z1, z2):
    half_sqrt3 = 0.5 * math.sqrt(3.0)
    sx, sy = z1[0] + z2[0], z1[1] + z2[1]
    dx, dy = z1[0] - z2[0], z1[1] - z2[1]
    mx, my = z0[0] - 0.5 * sx, z0[1] - 0.5 * sy
    return ((z0[0] + sx, z0[1] + sy),
            (mx + half_sqrt3 * dy, my - half_sqrt3 * dx),
            (mx - half_sqrt3 * dy, my + half_sqrt3 * dx))


def _butterfly_kernel(uc_ref, us_ref, tw_ref, p_ref, q_ref):
    lane_blocks = GROUP_DIM // 128

    def chunk(c, carry):
        r0 = pl.multiple_of(c * 16, 16)
        rows = pl.ds(r0, 16)
        tw = tw_ref[rows, :]
        row_id = r0 + lax.broadcasted_iota(jnp.int32, (16, 1), 0)
        valid = row_id < DFT_LEN
        for lb in range(lane_blocks):
            lanes = slice(lb * 128, (lb + 1) * 128)
            z = [(uc_ref[k, rows, lanes].astype(F32), -us_ref[k, rows, lanes].astype(F32))
                 for k in range(RADIX)]
            inner = []
            for q0 in range(3):
                outs = _dft3(z[q0], z[q0 + 3], z[q0 + 6])
                row = []
                for r0_ in range(3):
                    x, y = outs[r0_]
                    if r0_ * q0:
                        ang = -2.0 * math.pi * r0_ * q0 / 9.0
                        ct, st = math.cos(ang), math.sin(ang)
                        x, y = ct * x - st * y, ct * y + st * x
                    row.append((x, y))
                inner.append(row)
            for r0_ in range(3):
                outs = _dft3(inner[0][r0_], inner[1][r0_], inner[2][r0_])
                for r1 in range(3):
                    r = r0_ + 3 * r1
                    x, y = outs[r1]
                    if r == 0:
                        p, q = x, y
                    else:
                        cr = tw[:, 2 * r - 2:2 * r - 1]
                        sr = tw[:, 2 * r - 1:2 * r]
                        p = cr * x + sr * y
                        q = cr * y - sr * x
                    out_lanes = slice(r * GROUP_DIM + lb * 128, r * GROUP_DIM + (lb + 1) * 128)
                    p_ref[rows, out_lanes] = jnp.where(valid, p, 0.0).astype(BF16)
                    q_ref[rows, out_lanes] = jnp.where(valid, q, 0.0).astype(BF16)
        return carry

    lax.fori_loop(0, DFT_K // 16, chunk, 0, unroll=2)


def _butterfly_call(uc, us, tw):
    in_spec = pl.BlockSpec((None, None, RADIX, DFT_K, GROUP_DIM), lambda g, b: (g, b, 0, 0, 0))
    out_spec = pl.BlockSpec((None, None, DFT_K, RADIX * GROUP_DIM), lambda g, b: (g, b, 0, 0))
    out = jax.ShapeDtypeStruct((GROUPS, BATCH, DFT_K, RADIX * GROUP_DIM), BF16)
    return pl.pallas_call(
        _butterfly_kernel,
        grid=(GROUPS, BATCH),
        in_specs=[in_spec, in_spec, pl.BlockSpec((DFT_K, TW_COLS), lambda g, b: (0, 0))],
        out_specs=[out_spec, out_spec],
        out_shape=[out, out],
        compiler_params=_params(("parallel", "parallel")),
        name="dft_butterfly",
    )(uc, us, tw)


def _dft_kernel(c_ref, s_ref, p_ref, q_ref, o_ref, *, scale):
    acc = jnp.dot(c_ref[...], p_ref[...], preferred_element_type=F32)
    acc = acc + jnp.dot(s_ref[...], q_ref[...], preferred_element_type=F32)
    for r in range(DFT_NSPLIT):
        o_ref[r] = (acc[:, r * GROUP_DIM:(r + 1) * GROUP_DIM] * scale).astype(o_ref.dtype)


def _dft_call(cm, sm, p, q):
    width = DFT_NSPLIT * GROUP_DIM
    in_spec = pl.BlockSpec((None, None, DFT_K, width), lambda g, b, j: (g, b, 0, j))
    return pl.pallas_call(
        functools.partial(_dft_kernel, scale=1.0 / math.sqrt(SEQ_LEN * GROUP_DIM)),
        grid=(GROUPS, BATCH, RADIX // DFT_NSPLIT),
        in_specs=[_resident((DFT_LEN, DFT_K)), _resident((DFT_LEN, DFT_K)), in_spec, in_spec],
        out_specs=pl.BlockSpec((None, None, DFT_NSPLIT, DFT_LEN, GROUP_DIM),
                               lambda g, b, j: (g, b, j, 0, 0)),
        out_shape=jax.ShapeDtypeStruct((GROUPS, BATCH, RADIX, DFT_LEN, GROUP_DIM), BF16),
        compiler_params=_params(("parallel", "parallel", "parallel")),
        name="dft_matmul",
    )(cm, sm, p, q)


def _even_out_kernel(h_ref, ya_ref, xm_ref, xp_ref, xnx_ref, cw_ref, cb_ref, lg_ref, lb_ref,
                     perm_ref, w_ref, g_ref, ho_ref, xo_ref, xpad, xshift, yperm, ycat):
    i = pl.program_id(0)
    first = (i % TILES_PER_SEQ) == 0
    last = (i % TILES_PER_SEQ) == TILES_PER_SEQ - 1
    xpad[0:HALO, :] = jnp.where(first, 0.0, xp_ref[...])
    xpad[HALO:HALO + TM_OUT, :] = xm_ref[...]
    xpad[HALO + TM_OUT:, :] = jnp.where(last, 0.0, xnx_ref[...])

    for g in range(GROUPS):
        for r in range(RADIX):
            yperm[r * DFT_ROWS_PER_TILE:(r + 1) * DFT_ROWS_PER_TILE,
                  g * GROUP_DIM:(g + 1) * GROUP_DIM] = ya_ref[g, r]
    ycat[:, :MIX_WIDTH] = jnp.dot(perm_ref[...], yperm[...],
                                  preferred_element_type=F32).astype(BF16)
    ho_ref[...] = h_ref[...] + jnp.dot(ycat[:, :MIX_WIDTH], w_ref[:MIX_WIDTH, :],
                                       preferred_element_type=F32)

    row_vregs = CONV_RC // SUBLANES
    for g in range(GROUPS):
        lanes = slice(g * GROUP_DIM, (g + 1) * GROUP_DIM)
        out_lanes = slice(MIX_WIDTH + g * GROUP_DIM, MIX_WIDTH + (g + 1) * GROUP_DIM)
        for s in range(SUBLANES):
            xshift[g, s] = xpad[s:s + SHIFT_ROWS, lanes]
        for c in range(TM_OUT // CONV_RC):
            r0 = c * CONV_RC
            acc = [None] * row_vregs
            for j in range(CONV_KERNEL):
                oq, os_ = divmod(HALO - CONV_PAD + j, SUBLANES)
                w = cw_ref[j, :, lanes]
                for r in range(row_vregs):
                    start = r0 + SUBLANES * (oq + r)
                    term = w * xshift[g, os_, start:start + SUBLANES, :]
                    acc[r] = term if j == 0 else acc[r] + term
            conv = jnp.concatenate(acc, axis=0) + cb_ref[:, lanes]
            mu = jnp.mean(conv, axis=-1, keepdims=True)
            dev = conv - mu
            var = jnp.mean(dev * dev, axis=-1, keepdims=True)
            y = dev * lax.rsqrt(var + EPS) * lg_ref[:, lanes] + lb_ref[:, lanes]
            y = y * jax.nn.sigmoid(y)
            ycat[r0:r0 + CONV_RC, out_lanes] = y.astype(BF16)
        ho_ref[...] += jnp.dot(ycat[:, out_lanes], w_ref[out_lanes, :],
                               preferred_element_type=F32)

    xo_ref[...] = _rms_gain(ho_ref[...], g_ref[...]).astype(xo_ref.dtype)


def _even_out_call(h, ya, glu, cw, cb, lg, lb, perm, w, g, layer):
    hb = TM_OUT // HALO
    n_halo = ROWS // HALO
    row = lambda i: (i, 0)
    const = lambda i: (0, 0)
    return pl.pallas_call(
        _even_out_kernel,
        grid=(ROWS // TM_OUT,),
        in_specs=[pl.BlockSpec((TM_OUT, D_MODEL), row),
                  pl.BlockSpec((GROUPS, None, RADIX, DFT_ROWS_PER_TILE, GROUP_DIM),
                               lambda i: (0, i // TILES_PER_SEQ, 0, i % TILES_PER_SEQ, 0)),
                  pl.BlockSpec((TM_OUT, MIX_WIDTH), row),
                  pl.BlockSpec((HALO, MIX_WIDTH), lambda i: (jnp.maximum(i * hb - 1, 0), 0)),
                  pl.BlockSpec((HALO, MIX_WIDTH),
                               lambda i: (jnp.minimum((i + 1) * hb, n_halo - 1), 0)),
                  pl.BlockSpec((CONV_KERNEL, SUBLANES, MIX_WIDTH), lambda i: (0, 0, 0)),
                  pl.BlockSpec((1, MIX_WIDTH), const),
                  pl.BlockSpec((1, MIX_WIDTH), const),
                  pl.BlockSpec((1, MIX_WIDTH), const),
                  pl.BlockSpec((TM_OUT, TM_OUT), const),
                  _resident_layer((D_MODEL, D_MODEL), layer),
                  pl.BlockSpec((1, D_MODEL), const)],
        out_specs=[pl.BlockSpec((TM_OUT, D_MODEL), row), pl.BlockSpec((TM_OUT, D_MODEL), row)],
        out_shape=[jax.ShapeDtypeStruct((ROWS, D_MODEL), F32),
                   jax.ShapeDtypeStruct((ROWS, D_MODEL), BF16)],
        scratch_shapes=[pltpu.VMEM((TM_OUT + 2 * HALO, MIX_WIDTH), F32),
                        pltpu.VMEM((GROUPS, SUBLANES, SHIFT_ROWS, GROUP_DIM), F32),
                        pltpu.VMEM((TM_OUT, MIX_WIDTH), BF16),
                        pltpu.VMEM((TM_OUT, D_MODEL), BF16)],
        compiler_params=_params(("parallel",)),
        name="even_out_proj",
    )(h, ya, glu, glu, glu, cw, cb, lg, lb, perm, w, g)


def _odd_in_kernel(xn_ref, wb_ref, wc_ref, wh_ref, gb_ref, p_ref):
    xn = xn_ref[...]
    gb_ref[...] = jnp.dot(xn, wb_ref[...], preferred_element_type=F32).astype(gb_ref.dtype)
    gc = jnp.dot(xn, wc_ref[...], preferred_element_type=F32)
    hin = jnp.dot(xn, wh_ref[...], preferred_element_type=F32)
    p_ref[...] = (gc * hin).astype(p_ref.dtype)


def _odd_in_call(xn, w, layer):
    n_chunks = D_MODEL // ODD_CHUNK
    out = jax.ShapeDtypeStruct((ROWS, D_MODEL), BF16)
    wspec = lambda part: pl.BlockSpec((None, D_MODEL, ODD_CHUNK),
                                      lambda i, n: (layer, 0, part * n_chunks + n))
    return pl.pallas_call(
        _odd_in_kernel,
        grid=(ROWS // TM_IN, n_chunks),
        in_specs=[pl.BlockSpec((TM_IN, D_MODEL), lambda i, n: (i, 0)),
                  wspec(0), wspec(1), wspec(2)],
        out_specs=[pl.BlockSpec((TM_IN, ODD_CHUNK), lambda i, n: (i, n)),
                   pl.BlockSpec((TM_IN, ODD_CHUNK), lambda i, n: (i, n))],
        out_shape=[out, out],
        compiler_params=_params(("parallel", "arbitrary")),
        name="odd_in_proj",
    )(xn, w, w, w)


def _odd_out_kernel(h_ref, gb_ref, pm_ref, pp_ref, pn_ref, cw_ref, w_ref, g_ref,
                    ho_ref, xo_ref, xmid, xprev, xnext, ybuf):
    i = pl.program_id(0)
    first = (i % TILES_PER_SEQ) == 0
    last = (i % TILES_PER_SEQ) == TILES_PER_SEQ - 1
    xmid[...] = pm_ref[...].astype(F32)
    xprev[0:1, :] = jnp.where(first, 0.0, pp_ref[HALO - 1:HALO, :].astype(F32))
    xprev[1:TM_OUT, :] = xmid[0:TM_OUT - 1, :]
    xnext[0:TM_OUT - 1, :] = xmid[1:TM_OUT, :]
    xnext[TM_OUT - 1:TM_OUT, :] = jnp.where(last, 0.0, pn_ref[0:1, :].astype(F32))

    n_blocks = 4
    width = D_MODEL // n_blocks
    for blk in range(n_blocks):
        lanes = slice(blk * width, (blk + 1) * width)
        for c in range(TM_OUT // 16):
            rows = slice(c * 16, (c + 1) * 16)
            conv = cw_ref[0:1, lanes] * xprev[rows, lanes]
            conv = conv + cw_ref[1:2, lanes] * xmid[rows, lanes]
            conv = conv + cw_ref[2:3, lanes] * xnext[rows, lanes]
            ybuf[rows, lanes] = (gb_ref[rows, lanes].astype(F32) * conv).astype(BF16)
        part = jnp.dot(ybuf[:, lanes], w_ref[lanes, :], preferred_element_type=F32)
        if blk == 0:
            ho_ref[...] = h_ref[...] + part
        else:
            ho_ref[...] += part

    xo_ref[...] = _rms_gain(ho_ref[...], g_ref[...]).astype(xo_ref.dtype)


def _odd_out_call(h, gb, p, cw, w, g, layer):
    hb = TM_OUT // HALO
    n_halo = ROWS // HALO
    row = lambda i: (i, 0)
    const = lambda i: (0, 0)
    return pl.pallas_call(
        _odd_out_kernel,
        grid=(ROWS // TM_OUT,),
        in_specs=[pl.BlockSpec((TM_OUT, D_MODEL), row),
                  pl.BlockSpec((TM_OUT, D_MODEL), row),
                  pl.BlockSpec((TM_OUT, D_MODEL), row),
                  pl.BlockSpec((HALO, D_MODEL), lambda i: (jnp.maximum(i * hb - 1, 0), 0)),
                  pl.BlockSpec((HALO, D_MODEL),
                               lambda i: (jnp.minimum((i + 1) * hb, n_halo - 1), 0)),
                  pl.BlockSpec((3, D_MODEL), const),
                  _resident_layer((D_MODEL, D_MODEL), layer),
                  pl.BlockSpec((1, D_MODEL), const)],
        out_specs=[pl.BlockSpec((TM_OUT, D_MODEL), row), pl.BlockSpec((TM_OUT, D_MODEL), row)],
        out_shape=[jax.ShapeDtypeStruct((ROWS, D_MODEL), F32),
                   jax.ShapeDtypeStruct((ROWS, D_MODEL), BF16)],
        scratch_shapes=[pltpu.VMEM((TM_OUT, D_MODEL), F32),
                        pltpu.VMEM((TM_OUT, D_MODEL), F32),
                        pltpu.VMEM((TM_OUT, D_MODEL), F32),
                        pltpu.VMEM((TM_OUT, D_MODEL), BF16)],
        compiler_params=_params(("parallel",)),
        name="odd_out_proj",
    )(h, gb, p, p, p, cw, w, g)


def _mlp_accumulate(h_hbm, xn_ref, w1_hbm, w2_hbm, acc_ref, w1buf, w2buf, actbuf, wsem, hsem,
                    layer):
    i = pl.program_id(0)
    more_tiles = i + 1 < pl.num_programs(0)

    def slot_of(k):
        return k % 2 if isinstance(k, int) else lax.rem(k, 2)

    def chunk_of(k):
        start = k * TF_MLP if isinstance(k, int) else pl.multiple_of(k * TF_MLP, TF_MLP)
        return pl.ds(start, TF_MLP)

    def w1_copy(k):
        s = slot_of(k)
        return pltpu.make_async_copy(w1_hbm.at[layer, :, chunk_of(k)], w1buf.at[s], wsem.at[0, s])

    def w2_copy(k):
        s = slot_of(k)
        return pltpu.make_async_copy(w2_hbm.at[layer, chunk_of(k), :], w2buf.at[s], wsem.at[1, s])

    def up(k):
        s = slot_of(k)
        mid = jnp.dot(xn_ref[...], w1buf[s], preferred_element_type=F32)
        mid = jnp.maximum(mid, 0.0)
        actbuf[s] = (mid * mid).astype(BF16)

    def down(k):
        s = slot_of(k)
        acc_ref[...] += jnp.dot(actbuf[s], w2buf[s], preferred_element_type=F32)

    rows = pl.ds(pl.multiple_of(i * TM_MLP, 16), TM_MLP)
    h_copy = pltpu.make_async_copy(h_hbm.at[rows, :], acc_ref, hsem.at[0])
    h_copy.start()

    @pl.when(i == 0)
    def _():
        w1_copy(0).start()
        w1_copy(1).start()
        w2_copy(0).start()

    w1_copy(0).wait()
    up(0)

    w1_copy(2).start()
    w2_copy(1).start()
    w1_copy(1).wait()
    up(1)
    h_copy.wait()
    w2_copy(0).wait()
    down(0)

    def step(f, carry):
        @pl.when(f + 2 < NF_MLP)
        def _():
            w1_copy(f + 2).start()

        @pl.when(jnp.logical_and(f + 2 == NF_MLP, more_tiles))
        def _():
            w1_copy(0).start()

        w2_copy(f + 1).start()
        w1_copy(f + 1).wait()
        w2_copy(f).wait()
        up(f + 1)
        down(f)
        return carry

    lax.fori_loop(1, NF_MLP - 1, step, 0)

    @pl.when(more_tiles)
    def _():
        w1_copy(1).start()
        w2_copy(0).start()

    w2_copy(NF_MLP - 1).wait()

    def last_down(rows):
        s = slot_of(NF_MLP - 1)
        acc_ref[rows, :] += jnp.dot(actbuf[s, rows, :], w2buf[s], preferred_element_type=F32)

    return last_down


def _mlp_kernel(h_hbm, xn_ref, w1_hbm, w2_hbm, g_ref, ho_ref, xo_ref, w1buf, w2buf, actbuf, wsem,
                hsem, *, layer):
    last_down = _mlp_accumulate(h_hbm, xn_ref, w1_hbm, w2_hbm, ho_ref, w1buf, w2buf, actbuf, wsem,
                                hsem, layer)
    for rows in MLP_FINISH_BLOCKS:
        last_down(rows)
        xo_ref[rows, :] = _rms_gain(ho_ref[rows, :], g_ref[...]).astype(xo_ref.dtype)


def _mlp_final_kernel(h_hbm, xn_ref, w1_hbm, w2_hbm, g_ref, out_hbm, w1buf, w2buf, actbuf, wsem,
                      hsem, ho_ref, obuf, osem, *, layer):
    last_down = _mlp_accumulate(h_hbm, xn_ref, w1_hbm, w2_hbm, ho_ref, w1buf, w2buf, actbuf, wsem,
                                hsem, layer)
    i = pl.program_id(0)
    tiles_per_seq = SEQ_LEN // TM_MLP
    b = i // tiles_per_seq
    ti = i % tiles_per_seq
    body_rows = TM_MLP - N_META
    body = pltpu.make_async_copy(
        obuf.at[pl.ds(N_META, body_rows), :],
        out_hbm.at[b, pl.ds(pl.multiple_of(ti * TM_MLP, 16), body_rows), :], osem.at[0])
    head_start = pl.multiple_of(jnp.maximum(ti * TM_MLP - N_META, 0), 16)
    head = pltpu.make_async_copy(
        obuf.at[pl.ds(0, N_META), :], out_hbm.at[b, pl.ds(head_start, N_META), :], osem.at[1])

    @pl.when(i > 0)
    def _():
        body.wait()

    @pl.when(jnp.logical_and(i > 0, (i - 1) % tiles_per_seq > 0))
    def _():
        head.wait()

    for rows in MLP_FINISH_BLOCKS:
        last_down(rows)
        obuf[rows, :] = _rms_gain(ho_ref[rows, :], g_ref[...])
    body.start()

    @pl.when(ti > 0)
    def _():
        head.start()

    @pl.when(i == pl.num_programs(0) - 1)
    def _():
        body.wait()
        head.wait()


def _mlp_call(h, xn, w1, w2, g, layer, final):
    row = lambda i: (i, 0)
    scratch = [pltpu.VMEM((2, D_MODEL, TF_MLP), BF16),
               pltpu.VMEM((2, TF_MLP, D_MODEL), BF16),
               pltpu.VMEM((2, TM_MLP, TF_MLP), BF16),
               pltpu.SemaphoreType.DMA((2, 2)),
               pltpu.SemaphoreType.DMA((1,))]
    if final:
        body = _mlp_final_kernel
        out_specs = pl.BlockSpec(memory_space=pl.ANY)
        out_shape = jax.ShapeDtypeStruct((BATCH, SEQ, D_MODEL), F32)
        scratch += [pltpu.VMEM((TM_MLP, D_MODEL), F32), pltpu.VMEM((TM_MLP, D_MODEL), F32),
                    pltpu.SemaphoreType.DMA((2,))]
    else:
        body = _mlp_kernel
        out_specs = [pl.BlockSpec((TM_MLP, D_MODEL), row), pl.BlockSpec((TM_MLP, D_MODEL), row)]
        out_shape = [jax.ShapeDtypeStruct((ROWS, D_MODEL), F32),
                     jax.ShapeDtypeStruct((ROWS, D_MODEL), BF16)]
    return pl.pallas_call(
        functools.partial(body, layer=layer),
        grid=(ROWS // TM_MLP,),
        in_specs=[pl.BlockSpec(memory_space=pl.ANY),
                  pl.BlockSpec((TM_MLP, D_MODEL), row),
                  pl.BlockSpec(memory_space=pl.ANY),
                  pl.BlockSpec(memory_space=pl.ANY),
                  pl.BlockSpec((1, D_MODEL), lambda i: (0, 0))],
        out_specs=out_specs,
        out_shape=out_shape,
        scratch_shapes=scratch,
        compiler_params=_params(("arbitrary",)),
        name="sq_relu_mlp_final" if final else "sq_relu_mlp",
    )(h, xn, w1, w2, g)


def _channel_dft_table():
    c = jnp.arange(GROUP_DIM, dtype=jnp.int32)
    ang = ((c[:, None] * c[None, :]) % GROUP_DIM).astype(F32) * (2.0 * math.pi / GROUP_DIM)
    return jnp.concatenate([jnp.cos(ang), jnp.sin(ang)], axis=1).astype(BF16)


def _sequence_dft_tables():
    m = jnp.arange(DFT_LEN, dtype=jnp.int32)[:, None]
    k = jnp.arange(DFT_K, dtype=jnp.int32)[None, :]
    ang = ((m * k) % DFT_LEN).astype(F32) * (2.0 * math.pi / DFT_LEN)
    live = k < DFT_LEN
    cm = jnp.where(live, jnp.cos(ang), 0.0).astype(BF16)
    sm = jnp.where(live, jnp.sin(ang), 0.0).astype(BF16)
    return cm, sm


def _twiddle_table():
    k = jnp.arange(DFT_K, dtype=jnp.int32)
    cols = []
    for r in range(1, RADIX):
        ang = (r * k).astype(F32) * (2.0 * math.pi / SEQ_LEN)
        cols += [jnp.cos(ang), jnp.sin(ang)]
    tw = jnp.stack(cols, axis=1)
    return jnp.where((k < DFT_LEN)[:, None], tw, 0.0)


def _interleave_permutation():
    j = jnp.arange(TM_OUT, dtype=jnp.int32)[:, None]
    c = jnp.arange(TM_OUT, dtype=jnp.int32)[None, :]
    src = (j % RADIX) * DFT_ROWS_PER_TILE + j // RADIX
    return (c == src).astype(BF16)


def kernel(x, meta_tokens, norm_mix_g, norm_mlp_g, norm_final_g, ab_w_in, ab_w_out, ab_conv_w,
           ab_conv_b, ab_ln_g, ab_ln_b, c_w_in, c_conv_w, c_w_out, mlp_w1, mlp_w2):
    depth = norm_mix_g.shape[0]
    cs = _channel_dft_table()
    cm, sm = _sequence_dft_tables()
    tw = _twiddle_table()
    perm = _interleave_permutation()

    ab_w_in, ab_w_out, c_w_in, c_w_out, mlp_w1, mlp_w2 = (
        w.astype(BF16) for w in (ab_w_in, ab_w_out, c_w_in, c_w_out, mlp_w1, mlp_w2))

    h, xn = _prologue_call(x, meta_tokens.astype(x.dtype), norm_mix_g[0][None, :])
    for layer in range(depth):
        i = layer // 2
        g_mlp = norm_mlp_g[layer][None, :]
        if layer % 2 == 0:
            uc, us, glu = _even_in_call(xn, ab_w_in, cs, i)
            shape5 = (GROUPS, BATCH, RADIX, DFT_LEN, GROUP_DIM)
            p, q = _butterfly_call(uc.reshape(shape5), us.reshape(shape5), tw)
            ya = _dft_call(cm, sm, p, q)
            cw = jnp.broadcast_to(ab_conv_w[i][:, None, :], (CONV_KERNEL, SUBLANES, MIX_WIDTH))
            h, xn = _even_out_call(h, ya, glu, cw, ab_conv_b[i][None, :], ab_ln_g[i][None, :],
                                   ab_ln_b[i][None, :], perm, ab_w_out, g_mlp, i)
        else:
            gb, p = _odd_in_call(xn, c_w_in, i)
            h, xn = _odd_out_call(h, gb, p, c_conv_w[i], c_w_out, g_mlp, i)
        final = layer == depth - 1
        g_next = norm_final_g if final else norm_mix_g[layer + 1]
        if final:
            return _mlp_call(h, xn, mlp_w1, mlp_w2, g_next[None, :], layer, True)
        h, xn = _mlp_call(h, xn, mlp_w1, mlp_w2, g_next[None, :], layer, False)
```

```python
import functools
import math

import jax
import jax.numpy as jnp
from jax import lax
from jax.experimental import pallas as pl
from jax.experimental.pallas import tpu as pltpu

F32 = jnp.float32
BF16 = jnp.bfloat16

D_MODEL = 2048
BATCH = 2
SEQ = 8192
N_META = 16
SEQ_LEN = SEQ + N_META
ROWS = BATCH * SEQ_LEN
GROUPS = 4
GROUP_DIM = 256
MIX_WIDTH = GROUPS * GROUP_DIM
CONV_KERNEL = 31
CONV_PAD = (CONV_KERNEL - 1) // 2
D_FF = 4 * D_MODEL
EPS = 1e-6

RADIX = 9
DFT_LEN = SEQ_LEN // RADIX
DFT_K = 1024
DFT_NSPLIT = 3
TW_COLS = 2 * (RADIX - 1)

TM_PRO = 912
TM_IN = 1824
TM_OUT = 432
TM_MLP = 912
TF_MLP = 1024
NF_MLP = D_FF // TF_MLP
MLP_FINISH_BLOCKS = (slice(0, 448), slice(448, TM_MLP))
ODD_CHUNK = 512
HALO = 16
SUBLANES = 8
CONV_RC = 48
SHIFT_ROWS = TM_OUT + 2 * HALO - SUBLANES
TILES_PER_SEQ = SEQ_LEN // TM_OUT
DFT_ROWS_PER_TILE = TM_OUT // RADIX
VMEM_LIMIT = 58 * 1024 * 1024

assert SEQ_LEN % TM_PRO == 0 and SEQ_LEN % TM_OUT == 0 and SEQ_LEN % TM_MLP == 0
assert ROWS % TM_IN == 0 and TM_IN % 16 == 0
assert TM_OUT % CONV_RC == 0 and TM_OUT % HALO == 0 and CONV_RC % 16 == 0
assert DFT_LEN % 16 == 0 and DFT_ROWS_PER_TILE % 16 == 0 and TM_OUT % RADIX == 0
assert DFT_K % 256 == 0 and RADIX % DFT_NSPLIT == 0 and NF_MLP % 2 == 0


def _params(sem):
    return pltpu.CompilerParams(dimension_semantics=sem, vmem_limit_bytes=VMEM_LIMIT)


def _rms_gain(x, g):
    ms = jnp.mean(x * x, axis=-1, keepdims=True)
    return x * lax.rsqrt(ms + EPS) * g


def _resident(shape):
    nd = len(shape)
    return pl.BlockSpec(shape, lambda *_: (0,) * nd, pipeline_mode=pl.Buffered(1))


def _resident_layer(shape, layer):
    nd = len(shape)
    return pl.BlockSpec((None,) + shape, lambda *_: (layer,) + (0,) * nd,
                        pipeline_mode=pl.Buffered(1))


def _prologue_kernel(x_hbm, meta_ref, g_ref, h_ref, xn_ref, xbuf, sem):
    i = pl.program_id(0)
    tiles_per_seq = SEQ_LEN // TM_PRO
    body_rows = TM_PRO - N_META

    def copies(t, slot):
        b = t // tiles_per_seq
        ti = t % tiles_per_seq
        body = pltpu.make_async_copy(
            x_hbm.at[b, pl.ds(pl.multiple_of(ti * TM_PRO, 16), body_rows), :],
            xbuf.at[slot, pl.ds(N_META, body_rows), :], sem.at[0, slot])
        head_start = pl.multiple_of(jnp.maximum(ti * TM_PRO - N_META, 0), 16)
        head = pltpu.make_async_copy(
            x_hbm.at[b, pl.ds(head_start, N_META), :],
            xbuf.at[slot, pl.ds(0, N_META), :], sem.at[1, slot])
        return body, head, ti

    def start(t, slot):
        body, head, ti = copies(t, slot)
        body.start()

        @pl.when(ti > 0)
        def _():
            head.start()

    slot = lax.rem(i, 2)

    @pl.when(i == 0)
    def _():
        start(0, 0)

    @pl.when(i + 1 < pl.num_programs(0))
    def _():
        start(i + 1, 1 - slot)

    body, head, ti = copies(i, slot)
    body.wait()

    @pl.when(ti > 0)
    def _():
        head.wait()

    @pl.when(ti == 0)
    def _():
        xbuf[slot, 0:N_META, :] = meta_ref[...]

    hval = xbuf[slot]
    h_ref[...] = hval
    xn_ref[...] = _rms_gain(hval, g_ref[...]).astype(xn_ref.dtype)


def _prologue_call(x, meta, g):
    row = lambda i: (i, 0)
    return pl.pallas_call(
        _prologue_kernel,
        grid=(ROWS // TM_PRO,),
        in_specs=[pl.BlockSpec(memory_space=pl.ANY),
                  pl.BlockSpec((N_META, D_MODEL), lambda i: (0, 0)),
                  pl.BlockSpec((1, D_MODEL), lambda i: (0, 0))],
        out_specs=[pl.BlockSpec((TM_PRO, D_MODEL), row), pl.BlockSpec((TM_PRO, D_MODEL), row)],
        out_shape=[jax.ShapeDtypeStruct((ROWS, D_MODEL), F32),
                   jax.ShapeDtypeStruct((ROWS, D_MODEL), BF16)],
        scratch_shapes=[pltpu.VMEM((2, TM_PRO, D_MODEL), F32), pltpu.SemaphoreType.DMA((2, 2))],
        compiler_params=_params(("arbitrary",)),
        name="prologue_concat_norm",
    )(x, meta, g)


def _even_in_kernel(xn_ref, wu_ref, wa_ref, wg_ref, cs_ref, uc_ref, us_ref, glu_ref, wcat):
    wcat[:, 0:GROUP_DIM] = wu_ref[...]
    wcat[:, GROUP_DIM:2 * GROUP_DIM] = wa_ref[...]
    wcat[:, 2 * GROUP_DIM:] = wg_ref[...]
    proj = jnp.dot(xn_ref[...], wcat[...], preferred_element_type=F32)
    u = proj[:, :GROUP_DIM].astype(BF16)
    cs = jnp.dot(u, cs_ref[...], preferred_element_type=F32)
    uc_ref[...] = cs[:, :GROUP_DIM].astype(BF16)
    us_ref[...] = cs[:, GROUP_DIM:].astype(BF16)
    a = proj[:, GROUP_DIM:2 * GROUP_DIM]
    gate = proj[:, 2 * GROUP_DIM:]
    glu_ref[...] = a * jax.nn.sigmoid(gate)


def _even_in_call(xn, w, cs, layer):
    grp = jax.ShapeDtypeStruct((GROUPS, ROWS, GROUP_DIM), BF16)
    wspec = lambda part: pl.BlockSpec((None, D_MODEL, GROUP_DIM),
                                      lambda i, g: (layer, 0, part * GROUPS + g))
    return pl.pallas_call(
        _even_in_kernel,
        grid=(ROWS // TM_IN, GROUPS),
        in_specs=[pl.BlockSpec((TM_IN, D_MODEL), lambda i, g: (i, 0)),
                  wspec(0), wspec(1), wspec(2),
                  pl.BlockSpec((GROUP_DIM, 2 * GROUP_DIM), lambda i, g: (0, 0))],
        out_specs=[pl.BlockSpec((None, TM_IN, GROUP_DIM), lambda i, g: (g, i, 0)),
                   pl.BlockSpec((None, TM_IN, GROUP_DIM), lambda i, g: (g, i, 0)),
                   pl.BlockSpec((TM_IN, GROUP_DIM), lambda i, g: (i, g))],
        out_shape=[grp, grp, jax.ShapeDtypeStruct((ROWS, MIX_WIDTH), F32)],
        scratch_shapes=[pltpu.VMEM((D_MODEL, 3 * GROUP_DIM), BF16)],
        compiler_params=_params(("parallel", "arbitrary")),
        name="even_in_proj",
    )(xn, w, w, w, cs)


def _dft3(z0, z1, z2):
    half_sqrt3 = 0.5 * math.sqrt(3.0)
    sx, sy = z1[0] + z2[0], z1[1] + z2[1]
    dx, dy = z1[0] - z2[0], z1[1] - z2[1]
    mx, my = z0[0] - 0.5 * sx, z0[1] - 0.5 * sy
    return ((z0[0] + sx, z0[1] + sy),
            (mx + half_sqrt3 * dy, my - half_sqrt3 * dx),
            (mx - half_sqrt3 * dy, my + half_sqrt3 * dx))


def _butterfly_kernel(uc_ref, us_ref, tw_ref, p_ref, q_ref):
    lane_blocks = GROUP_DIM // 128

    def chunk(c, carry):
        r0 = pl.multiple_of(c * 16, 16)
        rows = pl.ds(r0, 16)
        tw = tw_ref[rows, :]
        row_id = r0 + lax.broadcasted_iota(jnp.int32, (16, 1), 0)
        valid = row_id < DFT_LEN
        for lb in range(lane_blocks):
            lanes = slice(lb * 128, (lb + 1) * 128)
            z = [(uc_ref[k, rows, lanes].astype(F32), -us_ref[k, rows, lanes].astype(F32))
                 for k in range(RADIX)]
            inner = []
            for q0 in range(3):
                outs = _dft3(z[q0], z[q0 + 3], z[q0 + 6])
                row = []
                for r0_ in range(3):
                    x, y = outs[r0_]
                    if r0_ * q0:
                        ang = -2.0 * math.pi * r0_ * q0 / 9.0
                        ct, st = math.cos(ang), math.sin(ang)
                        x, y = ct * x - st * y, ct * y + st * x
                    row.append((x, y))
                inner.append(row)
            for r0_ in range(3):
                outs = _dft3(inner[0][r0_], inner[1][r0_], inner[2][r0_])
                for r1 in range(3):
                    r = r0_ + 3 * r1
                    x, y = outs[r1]
                    if r == 0:
                        p, q = x, y
                    else:
                        cr = tw[:, 2 * r - 2:2 * r - 1]
                        sr = tw[:, 2 * r - 1:2 * r]
                        p = cr * x + sr * y
                        q = cr * y - sr * x
                    out_lanes = slice(r * GROUP_DIM + lb * 128, r * GROUP_DIM + (lb + 1) * 128)
                    p_ref[rows, out_lanes] = jnp.where(valid, p, 0.0).astype(BF16)
                    q_ref[rows, out_lanes] = jnp.where(valid, q, 0.0).astype(BF16)
        return carry

    lax.fori_loop(0, DFT_K // 16, chunk, 0, unroll=2)


def _butterfly_call(uc, us, tw):
    in_spec = pl.BlockSpec((None, None, RADIX, DFT_K, GROUP_DIM), lambda g, b: (g, b, 0, 0, 0))
    out_spec = pl.BlockSpec((None, None, DFT_K, RADIX * GROUP_DIM), lambda g, b: (g, b, 0, 0))
    out = jax.ShapeDtypeStruct((GROUPS, BATCH, DFT_K, RADIX * GROUP_DIM), BF16)
    return pl.pallas_call(
        _butterfly_kernel,
        grid=(GROUPS, BATCH),
        in_specs=[in_spec, in_spec, pl.BlockSpec((DFT_K, TW_COLS), lambda g, b: (0, 0))],
        out_specs=[out_spec, out_spec],
        out_shape=[out, out],
        compiler_params=_params(("parallel", "parallel")),
        name="dft_butterfly",
    )(uc, us, tw)


def _dft_kernel(c_ref, s_ref, p_ref, q_ref, o_ref, *, scale):
    acc = jnp.dot(c_ref[...], p_ref[...], preferred_element_type=F32)
    acc = acc + jnp.dot(s_ref[...], q_ref[...], preferred_element_type=F32)
    for r in range(DFT_NSPLIT):
        o_ref[r] = (acc[:, r * GROUP_DIM:(r + 1) * GROUP_DIM] * scale).astype(o_ref.dtype)


def _dft_call(cm, sm, p, q):
    width = DFT_NSPLIT * GROUP_DIM
    in_spec = pl.BlockSpec((None, None, DFT_K, width), lambda g, b, j: (g, b, 0, j))
    return pl.pallas_call(
        functools.partial(_dft_kernel, scale=1.0 / math.sqrt(SEQ_LEN * GROUP_DIM)),
        grid=(GROUPS, BATCH, RADIX // DFT_NSPLIT),
        in_specs=[_resident((DFT_LEN, DFT_K)), _resident((DFT_LEN, DFT_K)), in_spec, in_spec],
        out_specs=pl.BlockSpec((None, None, DFT_NSPLIT, DFT_LEN, GROUP_DIM),
                               lambda g, b, j: (g, b, j, 0, 0)),
        out_shape=jax.ShapeDtypeStruct((GROUPS, BATCH, RADIX, DFT_LEN, GROUP_DIM), BF16),
        compiler_params=_params(("parallel", "parallel", "parallel")),
        name="dft_matmul",
    )(cm, sm, p, q)


def _even_out_kernel(h_ref, ya_ref, xm_ref, xp_ref, xnx_ref, cw_ref, cb_ref, lg_ref, lb_ref,
                     perm_ref, w_ref, g_ref, ho_ref, xo_ref, xpad, xshift, yperm, ycat):
    i = pl.program_id(0)
    first = (i % TILES_PER_SEQ) == 0
    last = (i % TILES_PER_SEQ) == TILES_PER_SEQ - 1
    xpad[0:HALO, :] = jnp.where(first, 0.0, xp_ref[...])
    xpad[HALO:HALO + TM_OUT, :] = xm_ref[...]
    xpad[HALO + TM_OUT:, :] = jnp.where(last, 0.0, xnx_ref[...])

    for g in range(GROUPS):
        for r in range(RADIX):
            yperm[r * DFT_ROWS_PER_TILE:(r + 1) * DFT_ROWS_PER_TILE,
                  g * GROUP_DIM:(g + 1) * GROUP_DIM] = ya_ref[g, r]
    ycat[:, :MIX_WIDTH] = jnp.dot(perm_ref[...], yperm[...],
                                  preferred_element_type=F32).astype(BF16)
    ho_ref[...] = h_ref[...] + jnp.dot(ycat[:, :MIX_WIDTH], w_ref[:MIX_WIDTH, :],
                                       preferred_element_type=F32)

    row_vregs = CONV_RC // SUBLANES
    for g in range(GROUPS):
        lanes = slice(g * GROUP_DIM, (g + 1) * GROUP_DIM)
        out_lanes = slice(MIX_WIDTH + g * GROUP_DIM, MIX_WIDTH + (g + 1) * GROUP_DIM)
        for s in range(SUBLANES):
            xshift[g, s] = xpad[s:s + SHIFT_ROWS, lanes]
        for c in range(TM_OUT // CONV_RC):
            r0 = c * CONV_RC
            acc = [None] * row_vregs
            for j in range(CONV_KERNEL):
                oq, os_ = divmod(HALO - CONV_PAD + j, SUBLANES)
                w = cw_ref[j, :, lanes]
                for r in range(row_vregs):
                    start = r0 + SUBLANES * (oq + r)
                    term = w * xshift[g, os_, start:start + SUBLANES, :]
                    acc[r] = term if j == 0 else acc[r] + term
            conv = jnp.concatenate(acc, axis=0) + cb_ref[:, lanes]
            mu = jnp.mean(conv, axis=-1, keepdims=True)
            dev = conv - mu
            var = jnp.mean(dev * dev, axis=-1, keepdims=True)
            y = dev * lax.rsqrt(var + EPS) * lg_ref[:, lanes] + lb_ref[:, lanes]
            y = y * jax.nn.sigmoid(y)
            ycat[r0:r0 + CONV_RC, out_lanes] = y.astype(BF16)
        ho_ref[...] += jnp.dot(ycat[:, out_lanes], w_ref[out_lanes, :],
                               preferred_element_type=F32)

    xo_ref[...] = _rms_gain(ho_ref[...], g_ref[...]).astype(xo_ref.dtype)


def _even_out_call(h, ya, glu, cw, cb, lg, lb, perm, w, g, layer):
    hb = TM_OUT // HALO
    n_halo = ROWS // HALO
    row = lambda i: (i, 0)
    const = lambda i: (0, 0)
    return pl.pallas_call(
        _even_out_kernel,
        grid=(ROWS // TM_OUT,),
        in_specs=[pl.BlockSpec((TM_OUT, D_MODEL), row),
                  pl.BlockSpec((GROUPS, None, RADIX, DFT_ROWS_PER_TILE, GROUP_DIM),
                               lambda i: (0, i // TILES_PER_SEQ, 0, i % TILES_PER_SEQ, 0)),
                  pl.BlockSpec((TM_OUT, MIX_WIDTH), row),
                  pl.BlockSpec((HALO, MIX_WIDTH), lambda i: (jnp.maximum(i * hb - 1, 0), 0)),
                  pl.BlockSpec((HALO, MIX_WIDTH),
                               lambda i: (jnp.minimum((i + 1) * hb, n_halo - 1), 0)),
                  pl.BlockSpec((CONV_KERNEL, SUBLANES, MIX_WIDTH), lambda i: (0, 0, 0)),
                  pl.BlockSpec((1, MIX_WIDTH), const),
                  pl.BlockSpec((1, MIX_WIDTH), const),
                  pl.BlockSpec((1, MIX_WIDTH), const),
                  pl.BlockSpec((TM_OUT, TM_OUT), const),
                  _resident_layer((D_MODEL, D_MODEL), layer),
                  pl.BlockSpec((1, D_MODEL), const)],
        out_specs=[pl.BlockSpec((TM_OUT, D_MODEL), row), pl.BlockSpec((TM_OUT, D_MODEL), row)],
        out_shape=[jax.ShapeDtypeStruct((ROWS, D_MODEL), F32),
                   jax.ShapeDtypeStruct((ROWS, D_MODEL), BF16)],
        scratch_shapes=[pltpu.VMEM((TM_OUT + 2 * HALO, MIX_WIDTH), F32),
                        pltpu.VMEM((GROUPS, SUBLANES, SHIFT_ROWS, GROUP_DIM), F32),
                        pltpu.VMEM((TM_OUT, MIX_WIDTH), BF16),
                        pltpu.VMEM((TM_OUT, D_MODEL), BF16)],
        compiler_params=_params(("parallel",)),
        name="even_out_proj",
    )(h, ya, glu, glu, glu, cw, cb, lg, lb, perm, w, g)


def _odd_in_kernel(xn_ref, wb_ref, wc_ref, wh_ref, gb_ref, p_ref):
    xn = xn_ref[...]
    gb_ref[...] = jnp.dot(xn, wb_ref[...], preferred_element_type=F32).astype(gb_ref.dtype)
    gc = jnp.dot(xn, wc_ref[...], preferred_element_type=F32)
    hin = jnp.dot(xn, wh_ref[...], preferred_element_type=F32)
    p_ref[...] = (gc * hin).astype(p_ref.dtype)


def _odd_in_call(xn, w, layer):
    n_chunks = D_MODEL // ODD_CHUNK
    out = jax.ShapeDtypeStruct((ROWS, D_MODEL), BF16)
    wspec = lambda part: pl.BlockSpec((None, D_MODEL, ODD_CHUNK),
                                      lambda i, n: (layer, 0, part * n_chunks + n))
    return pl.pallas_call(
        _odd_in_kernel,
        grid=(ROWS // TM_IN, n_chunks),
        in_specs=[pl.BlockSpec((TM_IN, D_MODEL), lambda i, n: (i, 0)),
                  wspec(0), wspec(1), wspec(2)],
        out_specs=[pl.BlockSpec((TM_IN, ODD_CHUNK), lambda i, n: (i, n)),
                   pl.BlockSpec((TM_IN, ODD_CHUNK), lambda i, n: (i, n))],
        out_shape=[out, out],
        compiler_params=_params(("parallel", "arbitrary")),
        name="odd_in_proj",
    )(xn, w, w, w)


def _odd_out_kernel(h_ref, gb_ref, pm_ref, pp_ref, pn_ref, cw_ref, w_ref, g_ref,
                    ho_ref, xo_ref, xmid, xprev, xnext, ybuf):
    i = pl.program_id(0)
    first = (i % TILES_PER_SEQ) == 0
    last = (i % TILES_PER_SEQ) == TILES_PER_SEQ - 1
    xmid[...] = pm_ref[...].astype(F32)
    xprev[0:1, :] = jnp.where(first, 0.0, pp_ref[HALO - 1:HALO, :].astype(F32))
    xprev[1:TM_OUT, :] = xmid[0:TM_OUT - 1, :]
    xnext[0:TM_OUT - 1, :] = xmid[1:TM_OUT, :]
    xnext[TM_OUT - 1:TM_OUT, :] = jnp.where(last, 0.0, pn_ref[0:1, :].astype(F32))

    n_blocks = 4
    width = D_MODEL // n_blocks
    for blk in range(n_blocks):
        lanes = slice(blk * width, (blk + 1) * width)
        for c in range(TM_OUT // 16):
            rows = slice(c * 16, (c + 1) * 16)
            conv = cw_ref[0:1, lanes] * xprev[rows, lanes]
            conv = conv + cw_ref[1:2, lanes] * xmid[rows, lanes]
            conv = conv + cw_ref[2:3, lanes] * xnext[rows, lanes]
            ybuf[rows, lanes] = (gb_ref[rows, lanes].astype(F32) * conv).astype(BF16)
        part = jnp.dot(ybuf[:, lanes], w_ref[lanes, :], preferred_element_type=F32)
        if blk == 0:
            ho_ref[...] = h_ref[...] + part
        else:
            ho_ref[...] += part

    xo_ref[...] = _rms_gain(ho_ref[...], g_ref[...]).astype(xo_ref.dtype)


def _odd_out_call(h, gb, p, cw, w, g, layer):
    hb = TM_OUT // HALO
    n_halo = ROWS // HALO
    row = lambda i: (i, 0)
    const = lambda i: (0, 0)
    return pl.pallas_call(
        _odd_out_kernel,
        grid=(ROWS // TM_OUT,),
        in_specs=[pl.BlockSpec((TM_OUT, D_MODEL), row),
                  pl.BlockSpec((TM_OUT, D_MODEL), row),
                  pl.BlockSpec((TM_OUT, D_MODEL), row),
                  pl.BlockSpec((HALO, D_MODEL), lambda i: (jnp.maximum(i * hb - 1, 0), 0)),
                  pl.BlockSpec((HALO, D_MODEL),
                               lambda i: (jnp.minimum((i + 1) * hb, n_halo - 1), 0)),
                  pl.BlockSpec((3, D_MODEL), const),
                  _resident_layer((D_MODEL, D_MODEL), layer),
                  pl.BlockSpec((1, D_MODEL), const)],
        out_specs=[pl.BlockSpec((TM_OUT, D_MODEL), row), pl.BlockSpec((TM_OUT, D_MODEL), row)],
        out_shape=[jax.ShapeDtypeStruct((ROWS, D_MODEL), F32),
                   jax.ShapeDtypeStruct((ROWS, D_MODEL), BF16)],
        scratch_shapes=[pltpu.VMEM((TM_OUT, D_MODEL), F32),
                        pltpu.VMEM((TM_OUT, D_MODEL), F32),
                        pltpu.VMEM((TM_OUT, D_MODEL), F32),
                        pltpu.VMEM((TM_OUT, D_MODEL), BF16)],
        compiler_params=_params(("parallel",)),
        name="odd_out_proj",
    )(h, gb, p, p, p, cw, w, g)


def _mlp_accumulate(h_hbm, xn_ref, w1_hbm, w2_hbm, acc_ref, w1buf, w2buf, actbuf, wsem, hsem,
                    layer):
    i = pl.program_id(0)
    more_tiles = i + 1 < pl.num_programs(0)

    def slot_of(k):
        return k % 2 if isinstance(k, int) else lax.rem(k, 2)

    def chunk_of(k):
        start = k * TF_MLP if isinstance(k, int) else pl.multiple_of(k * TF_MLP, TF_MLP)
        return pl.ds(start, TF_MLP)

    def w1_copy(k):
        s = slot_of(k)
        return pltpu.make_async_copy(w1_hbm.at[layer, :, chunk_of(k)], w1buf.at[s], wsem.at[0, s])

    def w2_copy(k):
        s = slot_of(k)
        return pltpu.make_async_copy(w2_hbm.at[layer, chunk_of(k), :], w2buf.at[s], wsem.at[1, s])

    def up(k):
        s = slot_of(k)
        mid = jnp.dot(xn_ref[...], w1buf[s], preferred_element_type=F32)
        mid = jnp.maximum(mid, 0.0)
        actbuf[s] = (mid * mid).astype(BF16)

    def down(k):
        s = slot_of(k)
        acc_ref[...] += jnp.dot(actbuf[s], w2buf[s], preferred_element_type=F32)

    rows = pl.ds(pl.multiple_of(i * TM_MLP, 16), TM_MLP)
    h_copy = pltpu.make_async_copy(h_hbm.at[rows, :], acc_ref, hsem.at[0])
    h_copy.start()

    @pl.when(i == 0)
    def _():
        w1_copy(0).start()
        w1_copy(1).start()
        w2_copy(0).start()

    w1_copy(0).wait()
    up(0)

    w1_copy(2).start()
    w2_copy(1).start()
    w1_copy(1).wait()
    up(1)
    h_copy.wait()
    w2_copy(0).wait()
    down(0)

    def step(f, carry):
        @pl.when(f + 2 < NF_MLP)
        def _():
            w1_copy(f + 2).start()

        @pl.when(jnp.logical_and(f + 2 == NF_MLP, more_tiles))
        def _():
            w1_copy(0).start()

        w2_copy(f + 1).start()
        w1_copy(f + 1).wait()
        w2_copy(f).wait()
        up(f + 1)
        down(f)
        return carry

    lax.fori_loop(1, NF_MLP - 1, step, 0)

    @pl.when(more_tiles)
    def _():
        w1_copy(1).start()
        w2_copy(0).start()

    w2_copy(NF_MLP - 1).wait()

    def last_down(rows):
        s = slot_of(NF_MLP - 1)
        acc_ref[rows, :] += jnp.dot(actbuf[s, rows, :], w2buf[s], preferred_element_type=F32)

    return last_down


def _mlp_kernel(h_hbm, xn_ref, w1_hbm, w2_hbm, g_ref, ho_ref, xo_ref, w1buf, w2buf, actbuf, wsem,
                hsem, *, layer):
    last_down = _mlp_accumulate(h_hbm, xn_ref, w1_hbm, w2_hbm, ho_ref, w1buf, w2buf, actbuf, wsem,
                                hsem, layer)
    for rows in MLP_FINISH_BLOCKS:
        last_down(rows)
        xo_ref[rows, :] = _rms_gain(ho_ref[rows, :], g_ref[...]).astype(xo_ref.dtype)


def _mlp_final_kernel(h_hbm, xn_ref, w1_hbm, w2_hbm, g_ref, out_hbm, w1buf, w2buf, actbuf, wsem,
                      hsem, ho_ref, obuf, osem, *, layer):
    last_down = _mlp_accumulate(h_hbm, xn_ref, w1_hbm, w2_hbm, ho_ref, w1buf, w2buf, actbuf, wsem,
                                hsem, layer)
    i = pl.program_id(0)
    tiles_per_seq = SEQ_LEN // TM_MLP
    b = i // tiles_per_seq
    ti = i % tiles_per_seq
    body_rows = TM_MLP - N_META
    body = pltpu.make_async_copy(
        obuf.at[pl.ds(N_META, body_rows), :],
        out_hbm.at[b, pl.ds(pl.multiple_of(ti * TM_MLP, 16), body_rows), :], osem.at[0])
    head_start = pl.multiple_of(jnp.maximum(ti * TM_MLP - N_META, 0), 16)
    head = pltpu.make_async_copy(
        obuf.at[pl.ds(0, N_META), :], out_hbm.at[b, pl.ds(head_start, N_META), :], osem.at[1])

    @pl.when(i > 0)
    def _():
        body.wait()

    @pl.when(jnp.logical_and(i > 0, (i - 1) % tiles_per_seq > 0))
    def _():
        head.wait()

    for rows in MLP_FINISH_BLOCKS:
        last_down(rows)
        obuf[rows, :] = _rms_gain(ho_ref[rows, :], g_ref[...])
    body.start()

    @pl.when(ti > 0)
    def _():
        head.start()

    @pl.when(i == pl.num_programs(0) - 1)
    def _():
        body.wait()
        head.wait()


def _mlp_call(h, xn, w1, w2, g, layer, final):
    row = lambda i: (i, 0)
    scratch = [pltpu.VMEM((2, D_MODEL, TF_MLP), BF16),
               pltpu.VMEM((2, TF_MLP, D_MODEL), BF16),
               pltpu.VMEM((2, TM_MLP, TF_MLP), BF16),
               pltpu.SemaphoreType.DMA((2, 2)),
               pltpu.SemaphoreType.DMA((1,))]
    if final:
        body = _mlp_final_kernel
        out_specs = pl.BlockSpec(memory_space=pl.ANY)
        out_shape = jax.ShapeDtypeStruct((BATCH, SEQ, D_MODEL), F32)
        scratch += [pltpu.VMEM((TM_MLP, D_MODEL), F32), pltpu.VMEM((TM_MLP, D_MODEL), F32),
                    pltpu.SemaphoreType.DMA((2,))]
    else:
        body = _mlp_kernel
        out_specs = [pl.BlockSpec((TM_MLP, D_MODEL), row), pl.BlockSpec((TM_MLP, D_MODEL), row)]
        out_shape = [jax.ShapeDtypeStruct((ROWS, D_MODEL), F32),
                     jax.ShapeDtypeStruct((ROWS, D_MODEL), BF16)]
    return pl.pallas_call(
        functools.partial(body, layer=layer),
        grid=(ROWS // TM_MLP,),
        in_specs=[pl.BlockSpec(memory_space=pl.ANY),
                  pl.BlockSpec((TM_MLP, D_MODEL), row),
                  pl.BlockSpec(memory_space=pl.ANY),
                  pl.BlockSpec(memory_space=pl.ANY),
                  pl.BlockSpec((1, D_MODEL), lambda i: (0, 0))],
        out_specs=out_specs,
        out_shape=out_shape,
        scratch_shapes=scratch,
        compiler_params=_params(("arbitrary",)),
        name="sq_relu_mlp_final" if final else "sq_relu_mlp",
    )(h, xn, w1, w2, g)


def _channel_dft_table():
    c = jnp.arange(GROUP_DIM, dtype=jnp.int32)
    ang = ((c[:, None] * c[None, :]) % GROUP_DIM).astype(F32) * (2.0 * math.pi / GROUP_DIM)
    return jnp.concatenate([jnp.cos(ang), jnp.sin(ang)], axis=1).astype(BF16)


def _sequence_dft_tables():
    m = jnp.arange(DFT_LEN, dtype=jnp.int32)[:, None]
    k = jnp.arange(DFT_K, dtype=jnp.int32)[None, :]
    ang = ((m * k) % DFT_LEN).astype(F32) * (2.0 * math.pi / DFT_LEN)
    live = k < DFT_LEN
    cm = jnp.where(live, jnp.cos(ang), 0.0).astype(BF16)
    sm = jnp.where(live, jnp.sin(ang), 0.0).astype(BF16)
    return cm, sm


def _twiddle_table():
    k = jnp.arange(DFT_K, dtype=jnp.int32)
    cols = []
    for r in range(1, RADIX):
        ang = (r * k).astype(F32) * (2.0 * math.pi / SEQ_LEN)
        cols += [jnp.cos(ang), jnp.sin(ang)]
    tw = jnp.stack(cols, axis=1)
    return jnp.where((k < DFT_LEN)[:, None], tw, 0.0)


def _interleave_permutation():
    j = jnp.arange(TM_OUT, dtype=jnp.int32)[:, None]
    c = jnp.arange(TM_OUT, dtype=jnp.int32)[None, :]
    src = (j % RADIX) * DFT_ROWS_PER_TILE + j // RADIX
    return (c == src).astype(BF16)


def kernel(x, meta_tokens, norm_mix_g, norm_mlp_g, norm_final_g, ab_w_in, ab_w_out, ab_conv_w,
           ab_conv_b, ab_ln_g, ab_ln_b, c_w_in, c_conv_w, c_w_out, mlp_w1, mlp_w2):
    depth = norm_mix_g.shape[0]
    cs = _channel_dft_table()
    cm, sm = _sequence_dft_tables()
    tw = _twiddle_table()
    perm = _interleave_permutation()

    ab_w_in, ab_w_out, c_w_in, c_w_out, mlp_w1, mlp_w2 = (
        w.astype(BF16) for w in (ab_w_in, ab_w_out, c_w_in, c_w_out, mlp_w1, mlp_w2))

    h, xn = _prologue_call(x, meta_tokens.astype(x.dtype), norm_mix_g[0][None, :])
    for layer in range(depth):
        i = layer // 2
        g_mlp = norm_mlp_g[layer][None, :]
        if layer % 2 == 0:
            uc, us, glu = _even_in_call(xn, ab_w_in, cs, i)
            shape5 = (GROUPS, BATCH, RADIX, DFT_LEN, GROUP_DIM)
            p, q = _butterfly_call(uc.reshape(shape5), us.reshape(shape5), tw)
            ya = _dft_call(cm, sm, p, q)
            cw = jnp.broadcast_to(ab_conv_w[i][:, None, :], (CONV_KERNEL, SUBLANES, MIX_WIDTH))
            h, xn = _even_out_call(h, ya, glu, cw, ab_conv_b[i][None, :], ab_ln_g[i][None, :],
                                   ab_ln_b[i][None, :], perm, ab_w_out, g_mlp, i)
        else:
            gb, p = _odd_in_call(xn, c_w_in, i)
            h, xn = _odd_out_call(h, gb, p, c_conv_w[i], c_w_out, g_mlp, i)
        final = layer == depth - 1
        g_next = norm_final_g if final else norm_mix_g[layer + 1]
        if final:
            return _mlp_call(h, xn, mlp_w1, mlp_w2, g_next[None, :], layer, True)
        h, xn = _mlp_call(h, xn, mlp_w1, mlp_w2, g_next[None, :], layer, False)
```

```python
import functools
import math

import jax
import jax.numpy as jnp
from jax import lax
from jax.experimental import pallas as pl
from jax.experimental.pallas import tpu as pltpu

F32 = jnp.float32
BF16 = jnp.bfloat16

D_MODEL = 2048
BATCH = 2
SEQ = 8192
N_META = 16
SEQ_LEN = SEQ + N_META
ROWS = BATCH * SEQ_LEN
GROUPS = 4
GROUP_DIM = 256
MIX_WIDTH = GROUPS * GROUP_DIM
CONV_KERNEL = 31
CONV_PAD = (CONV_KERNEL - 1) // 2
D_FF = 4 * D_MODEL
EPS = 1e-6

RADIX = 9
DFT_LEN = SEQ_LEN // RADIX
DFT_K = 1024
DFT_NSPLIT = 3
TW_COLS = 2 * (RADIX - 1)

TM_PRO = 912
TM_IN = 1824
TM_OUT = 432
TM_MLP = 912
TF_MLP = 1024
NF_MLP = D_FF // TF_MLP
MLP_FINISH_BLOCKS = (slice(0, 448), slice(448, TM_MLP))
ODD_CHUNK = 512
HALO = 16
SUBLANES = 8
CONV_RC = 48
SHIFT_ROWS = TM_OUT + 2 * HALO - SUBLANES
TILES_PER_SEQ = SEQ_LEN // TM_OUT
DFT_ROWS_PER_TILE = TM_OUT // RADIX
VMEM_LIMIT = 58 * 1024 * 1024

assert SEQ_LEN % TM_PRO == 0 and SEQ_LEN % TM_OUT == 0 and SEQ_LEN % TM_MLP == 0
assert ROWS % TM_IN == 0 and TM_IN % 16 == 0
assert TM_OUT % CONV_RC == 0 and TM_OUT % HALO == 0 and CONV_RC % 16 == 0
assert DFT_LEN % 16 == 0 and DFT_ROWS_PER_TILE % 16 == 0 and TM_OUT % RADIX == 0
assert DFT_K % 256 == 0 and RADIX % DFT_NSPLIT == 0 and NF_MLP % 2 == 0


def _params(sem):
    return pltpu.CompilerParams(dimension_semantics=sem, vmem_limit_bytes=VMEM_LIMIT)


def _rms_gain(x, g):
    ms = jnp.mean(x * x, axis=-1, keepdims=True)
    return x * lax.rsqrt(ms + EPS) * g


def _resident(shape):
    nd = len(shape)
    return pl.BlockSpec(shape, lambda *_: (0,) * nd, pipeline_mode=pl.Buffered(1))


def _resident_layer(shape, layer):
    nd = len(shape)
    return pl.BlockSpec((None,) + shape, lambda *_: (layer,) + (0,) * nd,
                        pipeline_mode=pl.Buffered(1))


def _prologue_kernel(x_hbm, meta_ref, g_ref, h_ref, xn_ref, xbuf, sem):
    i = pl.program_id(0)
    tiles_per_seq = SEQ_LEN // TM_PRO
    body_rows = TM_PRO - N_META

    def copies(t, slot):
        b = t // tiles_per_seq
        ti = t % tiles_per_seq
        body = pltpu.make_async_copy(
            x_hbm.at[b, pl.ds(pl.multiple_of(ti * TM_PRO, 16), body_rows), :],
            xbuf.at[slot, pl.ds(N_META, body_rows), :], sem.at[0, slot])
        head_start = pl.multiple_of(jnp.maximum(ti * TM_PRO - N_META, 0), 16)
        head = pltpu.make_async_copy(
            x_hbm.at[b, pl.ds(head_start, N_META), :],
            xbuf.at[slot, pl.ds(0, N_META), :], sem.at[1, slot])
        return body, head, ti

    def start(t, slot):
        body, head, ti = copies(t, slot)
        body.start()

        @pl.when(ti > 0)
        def _():
            head.start()

    slot = lax.rem(i, 2)

    @pl.when(i == 0)
    def _():
        start(0, 0)

    @pl.when(i + 1 < pl.num_programs(0))
    def _():
        start(i + 1, 1 - slot)

    body, head, ti = copies(i, slot)
    body.wait()

    @pl.when(ti > 0)
    def _():
        head.wait()

    @pl.when(ti == 0)
    def _():
        xbuf[slot, 0:N_META, :] = meta_ref[...]

    hval = xbuf[slot]
    h_ref[...] = hval
    xn_ref[...] = _rms_gain(hval, g_ref[...]).astype(xn_ref.dtype)


def _prologue_call(x, meta, g):
    row = lambda i: (i, 0)
    return pl.pallas_call(
        _prologue_kernel,
        grid=(ROWS // TM_PRO,),
        in_specs=[pl.BlockSpec(memory_space=pl.ANY),
                  pl.BlockSpec((N_META, D_MODEL), lambda i: (0, 0)),
                  pl.BlockSpec((1, D_MODEL), lambda i: (0, 0))],
        out_specs=[pl.BlockSpec((TM_PRO, D_MODEL), row), pl.BlockSpec((TM_PRO, D_MODEL), row)],
        out_shape=[jax.ShapeDtypeStruct((ROWS, D_MODEL), F32),
                   jax.ShapeDtypeStruct((ROWS, D_MODEL), BF16)],
        scratch_shapes=[pltpu.VMEM((2, TM_PRO, D_MODEL), F32), pltpu.SemaphoreType.DMA((2, 2))],
        compiler_params=_params(("arbitrary",)),
        name="prologue_concat_norm",
    )(x, meta, g)


def _even_in_kernel(xn_ref, wu_ref, wa_ref, wg_ref, cs_ref, uc_ref, us_ref, glu_ref, wcat):
    wcat[:, 0:GROUP_DIM] = wu_ref[...]
    wcat[:, GROUP_DIM:2 * GROUP_DIM] = wa_ref[...]
    wcat[:, 2 * GROUP_DIM:] = wg_ref[...]
    proj = jnp.dot(xn_ref[...], wcat[...], preferred_element_type=F32)
    u = proj[:, :GROUP_DIM].astype(BF16)
    cs = jnp.dot(u, cs_ref[...], preferred_element_type=F32)
    uc_ref[...] = cs[:, :GROUP_DIM].astype(BF16)
    us_ref[...] = cs[:, GROUP_DIM:].astype(BF16)
    a = proj[:, GROUP_DIM:2 * GROUP_DIM]
    gate = proj[:, 2 * GROUP_DIM:]
    glu_ref[...] = a * jax.nn.sigmoid(gate)


def _even_in_call(xn, w, cs, layer):
    grp = jax.ShapeDtypeStruct((GROUPS, ROWS, GROUP_DIM), BF16)
    wspec = lambda part: pl.BlockSpec((None, D_MODEL, GROUP_DIM),
                                      lambda i, g: (layer, 0, part * GROUPS + g))
    return pl.pallas_call(
        _even_in_kernel,
        grid=(ROWS // TM_IN, GROUPS),
        in_specs=[pl.BlockSpec((TM_IN, D_MODEL), lambda i, g: (i, 0)),
                  wspec(0), wspec(1), wspec(2),
                  pl.BlockSpec((GROUP_DIM, 2 * GROUP_DIM), lambda i, g: (0, 0))],
        out_specs=[pl.BlockSpec((None, TM_IN, GROUP_DIM), lambda i, g: (g, i, 0)),
                   pl.BlockSpec((None, TM_IN, GROUP_DIM), lambda i, g: (g, i, 0)),
                   pl.BlockSpec((TM_IN, GROUP_DIM), lambda i, g: (i, g))],
        out_shape=[grp, grp, jax.ShapeDtypeStruct((ROWS, MIX_WIDTH), F32)],
        scratch_shapes=[pltpu.VMEM((D_MODEL, 3 * GROUP_DIM), BF16)],
        compiler_params=_params(("parallel", "arbitrary")),
        name="even_in_proj",
    )(xn, w, w, w, cs)


def _dft3(z0, z1, z2):
    half_sqrt3 = 0.5 * math.sqrt(3.0)
    sx, sy = z1[0] + z2[0], z1[1] + z2[1]
    dx, dy = z1[0] - z2[0], z1[1] - z2[1]
    mx, my = z0[0] - 0.5 * sx, z0[1] - 0.5 * sy
    return ((z0[0] + sx, z0[1] + sy),
            (mx + half_sqrt3 * dy, my - half_sqrt3 * dx),
            (mx - half_sqrt3 * dy, my + half_sqrt3 * dx))


def _butterfly_kernel(uc_ref, us_ref, tw_ref, p_ref, q_ref):
    lane_blocks = GROUP_DIM // 128

    def chunk(c, carry):
        r0 = pl.multiple_of(c * 16, 16)
        rows = pl.ds(r0, 16)
        tw = tw_ref[rows, :]
        for lb in range(lane_blocks):
            lanes = slice(lb * 128, (lb + 1) * 128)
            z = [(uc_ref[k, rows, lanes].astype(F32), -us_ref[k, rows, lanes].astype(F32))
                 for k in range(RADIX)]
            inner = []
            for q0 in range(3):
                outs = _dft3(z[q0], z[q0 + 3], z[q0 + 6])
                row = []
                for r0_ in range(3):
                    x, y = outs[r0_]
                    if r0_ * q0:
                        ang = -2.0 * math.pi * r0_ * q0 / 9.0
                        ct, st = math.cos(ang), math.sin(ang)
                        x, y = ct * x - st * y, ct * y + st * x
                    row.append((x, y))
                inner.append(row)
            for r0_ in range(3):
                outs = _dft3(inner[0][r0_], inner[1][r0_], inner[2][r0_])
                for r1 in range(3):
                    r = r0_ + 3 * r1
                    x, y = outs[r1]
                    if r == 0:
                        p, q = x, y
                    else:
                        cr = tw[:, 2 * r - 2:2 * r - 1]
                        sr = tw[:, 2 * r - 1:2 * r]
                        p = cr * x + sr * y
                        q = cr * y - sr * x
                    out_lanes = slice(r * GROUP_DIM + lb * 128, r * GROUP_DIM + (lb + 1) * 128)
                    p_ref[rows, out_lanes] = p.astype(BF16)
                    q_ref[rows, out_lanes] = q.astype(BF16)
        return carry

    lax.fori_loop(0, DFT_LEN // 16, chunk, 0, unroll=3)
    pad = jnp.zeros((DFT_K - DFT_LEN, RADIX * GROUP_DIM), BF16)
    p_ref[DFT_LEN:, :] = pad
    q_ref[DFT_LEN:, :] = pad


def _butterfly_call(uc, us, tw):
    in_spec = pl.BlockSpec((None, None, RADIX, DFT_K, GROUP_DIM), lambda g, b: (g, b, 0, 0, 0))
    out_spec = pl.BlockSpec((None, None, DFT_K, RADIX * GROUP_DIM), lambda g, b: (g, b, 0, 0))
    out = jax.ShapeDtypeStruct((GROUPS, BATCH, DFT_K, RADIX * GROUP_DIM), BF16)
    return pl.pallas_call(
        _butterfly_kernel,
        grid=(GROUPS, BATCH),
        in_specs=[in_spec, in_spec, pl.BlockSpec((DFT_K, TW_COLS), lambda g, b: (0, 0))],
        out_specs=[out_spec, out_spec],
        out_shape=[out, out],
        compiler_params=_params(("parallel", "parallel")),
        name="dft_butterfly",
    )(uc, us, tw)


def _dft_kernel(c_ref, s_ref, p_ref, q_ref, o_ref, *, scale):
    acc = jnp.dot(c_ref[...], p_ref[...], preferred_element_type=F32)
    acc = acc + jnp.dot(s_ref[...], q_ref[...], preferred_element_type=F32)
    for r in range(DFT_NSPLIT):
        o_ref[r] = (acc[:, r * GROUP_DIM:(r + 1) * GROUP_DIM] * scale).astype(o_ref.dtype)


def _dft_call(cm, sm, p, q):
    width = DFT_NSPLIT * GROUP_DIM
    in_spec = pl.BlockSpec((None, None, DFT_K, width), lambda g, b, j: (g, b, 0, j))
    return pl.pallas_call(
        functools.partial(_dft_kernel, scale=1.0 / math.sqrt(SEQ_LEN * GROUP_DIM)),
        grid=(GROUPS, BATCH, RADIX // DFT_NSPLIT),
        in_specs=[_resident((DFT_LEN, DFT_K)), _resident((DFT_LEN, DFT_K)), in_spec, in_spec],
        out_specs=pl.BlockSpec((None, None, DFT_NSPLIT, DFT_LEN, GROUP_DIM),
                               lambda g, b, j: (g, b, j, 0, 0)),
        out_shape=jax.ShapeDtypeStruct((GROUPS, BATCH, RADIX, DFT_LEN, GROUP_DIM), BF16),
        compiler_params=_params(("parallel", "parallel", "parallel")),
        name="dft_matmul",
    )(cm, sm, p, q)


def _even_out_kernel(h_ref, ya_ref, xm_ref, xp_ref, xnx_ref, cw_ref, cb_ref, lg_ref, lb_ref,
                     perm_ref, w_ref, g_ref, ho_ref, xo_ref, xpad, xshift, yperm, ycat):
    i = pl.program_id(0)
    first = (i % TILES_PER_SEQ) == 0
    last = (i % TILES_PER_SEQ) == TILES_PER_SEQ - 1
    xpad[0:HALO, :] = jnp.where(first, 0.0, xp_ref[...])
    xpad[HALO:HALO + TM_OUT, :] = xm_ref[...]
    xpad[HALO + TM_OUT:, :] = jnp.where(last, 0.0, xnx_ref[...])

    for g in range(GROUPS):
        for r in range(RADIX):
            yperm[r * DFT_ROWS_PER_TILE:(r + 1) * DFT_ROWS_PER_TILE,
                  g * GROUP_DIM:(g + 1) * GROUP_DIM] = ya_ref[g, r]
    ycat[:, :MIX_WIDTH] = jnp.dot(perm_ref[...], yperm[...],
                                  preferred_element_type=F32).astype(BF16)
    ho_ref[...] = h_ref[...] + jnp.dot(ycat[:, :MIX_WIDTH], w_ref[:MIX_WIDTH, :],
                                       preferred_element_type=F32)

    row_vregs = CONV_RC // SUBLANES
    for g in range(GROUPS):
        lanes = slice(g * GROUP_DIM, (g + 1) * GROUP_DIM)
        out_lanes = slice(MIX_WIDTH + g * GROUP_DIM, MIX_WIDTH + (g + 1) * GROUP_DIM)
        for s in range(SUBLANES):
            xshift[g, s] = xpad[s:s + SHIFT_ROWS, lanes]
        for c in range(TM_OUT // CONV_RC):
            r0 = c * CONV_RC
            acc = [None] * row_vregs
            for j in range(CONV_KERNEL):
                oq, os_ = divmod(HALO - CONV_PAD + j, SUBLANES)
                w = cw_ref[j, :, lanes]
                for r in range(row_vregs):
                    start = r0 + SUBLANES * (oq + r)
                    term = w * xshift[g, os_, start:start + SUBLANES, :]
                    acc[r] = term if j == 0 else acc[r] + term
            conv = jnp.concatenate(acc, axis=0) + cb_ref[:, lanes]
            mu = jnp.mean(conv, axis=-1, keepdims=True)
            dev = conv - mu
            var = jnp.mean(dev * dev, axis=-1, keepdims=True)
            y = dev * lax.rsqrt(var + EPS) * lg_ref[:, lanes] + lb_ref[:, lanes]
            y = y * jax.nn.sigmoid(y)
            ycat[r0:r0 + CONV_RC, out_lanes] = y.astype(BF16)
        ho_ref[...] += jnp.dot(ycat[:, out_lanes], w_ref[out_lanes, :],
                               preferred_element_type=F32)

    xo_ref[...] = _rms_gain(ho_ref[...], g_ref[...]).astype(xo_ref.dtype)


def _even_out_call(h, ya, glu, cw, cb, lg, lb, perm, w, g, layer):
    hb = TM_OUT // HALO
    n_halo = ROWS // HALO
    row = lambda i: (i, 0)
    const = lambda i: (0, 0)
    return pl.pallas_call(
        _even_out_kernel,
        grid=(ROWS // TM_OUT,),
        in_specs=[pl.BlockSpec((TM_OUT, D_MODEL), row),
                  pl.BlockSpec((GROUPS, None, RADIX, DFT_ROWS_PER_TILE, GROUP_DIM),
                               lambda i: (0, i // TILES_PER_SEQ, 0, i % TILES_PER_SEQ, 0)),
                  pl.BlockSpec((TM_OUT, MIX_WIDTH), row),
                  pl.BlockSpec((HALO, MIX_WIDTH), lambda i: (jnp.maximum(i * hb - 1, 0), 0)),
                  pl.BlockSpec((HALO, MIX_WIDTH),
                               lambda i: (jnp.minimum((i + 1) * hb, n_halo - 1), 0)),
                  pl.BlockSpec((CONV_KERNEL, SUBLANES, MIX_WIDTH), lambda i: (0, 0, 0)),
                  pl.BlockSpec((1, MIX_WIDTH), const),
                  pl.BlockSpec((1, MIX_WIDTH), const),
                  pl.BlockSpec((1, MIX_WIDTH), const),
                  pl.BlockSpec((TM_OUT, TM_OUT), const),
                  _resident_layer((D_MODEL, D_MODEL), layer),
                  pl.BlockSpec((1, D_MODEL), const)],
        out_specs=[pl.BlockSpec((TM_OUT, D_MODEL), row), pl.BlockSpec((TM_OUT, D_MODEL), row)],
        out_shape=[jax.ShapeDtypeStruct((ROWS, D_MODEL), F32),
                   jax.ShapeDtypeStruct((ROWS, D_MODEL), BF16)],
        scratch_shapes=[pltpu.VMEM((TM_OUT + 2 * HALO, MIX_WIDTH), F32),
                        pltpu.VMEM((GROUPS, SUBLANES, SHIFT_ROWS, GROUP_DIM), F32),
                        pltpu.VMEM((TM_OUT, MIX_WIDTH), BF16),
                        pltpu.VMEM((TM_OUT, D_MODEL), BF16)],
        compiler_params=_params(("parallel",)),
        name="even_out_proj",
    )(h, ya, glu, glu, glu, cw, cb, lg, lb, perm, w, g)


def _odd_in_kernel(xn_ref, wb_ref, wc_ref, wh_ref, gb_ref, p_ref):
    xn = xn_ref[...]
    gb_ref[...] = jnp.dot(xn, wb_ref[...], preferred_element_type=F32).astype(gb_ref.dtype)
    gc = jnp.dot(xn, wc_ref[...], preferred_element_type=F32)
    hin = jnp.dot(xn, wh_ref[...], preferred_element_type=F32)
    p_ref[...] = (gc * hin).astype(p_ref.dtype)


def _odd_in_call(xn, w, layer):
    n_chunks = D_MODEL // ODD_CHUNK
    out = jax.ShapeDtypeStruct((ROWS, D_MODEL), BF16)
    wspec = lambda part: pl.BlockSpec((None, D_MODEL, ODD_CHUNK),
                                      lambda i, n: (layer, 0, part * n_chunks + n))
    return pl.pallas_call(
        _odd_in_kernel,
        grid=(ROWS // TM_IN, n_chunks),
        in_specs=[pl.BlockSpec((TM_IN, D_MODEL), lambda i, n: (i, 0)),
                  wspec(0), wspec(1), wspec(2)],
        out_specs=[pl.BlockSpec((TM_IN, ODD_CHUNK), lambda i, n: (i, n)),
                   pl.BlockSpec((TM_IN, ODD_CHUNK), lambda i, n: (i, n))],
        out_shape=[out, out],
        compiler_params=_params(("parallel", "arbitrary")),
        name="odd_in_proj",
    )(xn, w, w, w)


def _odd_out_kernel(h_ref, gb_ref, pm_ref, pp_ref, pn_ref, cw_ref, w_ref, g_ref,
                    ho_ref, xo_ref, xmid, xprev, xnext, ybuf):
    i = pl.program_id(0)
    first = (i % TILES_PER_SEQ) == 0
    last = (i % TILES_PER_SEQ) == TILES_PER_SEQ - 1
    xmid[...] = pm_ref[...].astype(F32)
    xprev[0:1, :] = jnp.where(first, 0.0, pp_ref[HALO - 1:HALO, :].astype(F32))
    xprev[1:TM_OUT, :] = xmid[0:TM_OUT - 1, :]
    xnext[0:TM_OUT - 1, :] = xmid[1:TM_OUT, :]
    xnext[TM_OUT - 1:TM_OUT, :] = jnp.where(last, 0.0, pn_ref[0:1, :].astype(F32))

    n_blocks = 4
    width = D_MODEL // n_blocks
    for blk in range(n_blocks):
        lanes = slice(blk * width, (blk + 1) * width)
        for c in range(TM_OUT // 16):
            rows = slice(c * 16, (c + 1) * 16)
            conv = cw_ref[0:1, lanes] * xprev[rows, lanes]
            conv = conv + cw_ref[1:2, lanes] * xmid[rows, lanes]
            conv = conv + cw_ref[2:3, lanes] * xnext[rows, lanes]
            ybuf[rows, lanes] = (gb_ref[rows, lanes].astype(F32) * conv).astype(BF16)
        part = jnp.dot(ybuf[:, lanes], w_ref[lanes, :], preferred_element_type=F32)
        if blk == 0:
            ho_ref[...] = h_ref[...] + part
        else:
            ho_ref[...] += part

    xo_ref[...] = _rms_gain(ho_ref[...], g_ref[...]).astype(xo_ref.dtype)


def _odd_out_call(h, gb, p, cw, w, g, layer):
    hb = TM_OUT // HALO
    n_halo = ROWS // HALO
    row = lambda i: (i, 0)
    const = lambda i: (0, 0)
    return pl.pallas_call(
        _odd_out_kernel,
        grid=(ROWS // TM_OUT,),
        in_specs=[pl.BlockSpec((TM_OUT, D_MODEL), row),
                  pl.BlockSpec((TM_OUT, D_MODEL), row),
                  pl.BlockSpec((TM_OUT, D_MODEL), row),
                  pl.BlockSpec((HALO, D_MODEL), lambda i: (jnp.maximum(i * hb - 1, 0), 0)),
                  pl.BlockSpec((HALO, D_MODEL),
                               lambda i: (jnp.minimum((i + 1) * hb, n_halo - 1), 0)),
                  pl.BlockSpec((3, D_MODEL), const),
                  _resident_layer((D_MODEL, D_MODEL), layer),
                  pl.BlockSpec((1, D_MODEL), const)],
        out_specs=[pl.BlockSpec((TM_OUT, D_MODEL), row), pl.BlockSpec((TM_OUT, D_MODEL), row)],
        out_shape=[jax.ShapeDtypeStruct((ROWS, D_MODEL), F32),
                   jax.ShapeDtypeStruct((ROWS, D_MODEL), BF16)],
        scratch_shapes=[pltpu.VMEM((TM_OUT, D_MODEL), F32),
                        pltpu.VMEM((TM_OUT, D_MODEL), F32),
                        pltpu.VMEM((TM_OUT, D_MODEL), F32),
                        pltpu.VMEM((TM_OUT, D_MODEL), BF16)],
        compiler_params=_params(("parallel",)),
        name="odd_out_proj",
    )(h, gb, p, p, p, cw, w, g)


def _mlp_accumulate(h_hbm, xn_ref, w1_hbm, w2_hbm, acc_ref, w1buf, w2buf, actbuf, wsem, hsem,
                    layer):
    i = pl.program_id(0)
    more_tiles = i + 1 < pl.num_programs(0)

    def slot_of(k):
        return k % 2 if isinstance(k, int) else lax.rem(k, 2)

    def chunk_of(k):
        start = k * TF_MLP if isinstance(k, int) else pl.multiple_of(k * TF_MLP, TF_MLP)
        return pl.ds(start, TF_MLP)

    def w1_copy(k):
        s = slot_of(k)
        return pltpu.make_async_copy(w1_hbm.at[layer, :, chunk_of(k)], w1buf.at[s], wsem.at[0, s])

    def w2_copy(k):
        s = slot_of(k)
        return pltpu.make_async_copy(w2_hbm.at[layer, chunk_of(k), :], w2buf.at[s], wsem.at[1, s])

    def up(k):
        s = slot_of(k)
        mid = jnp.dot(xn_ref[...], w1buf[s], preferred_element_type=F32)
        mid = jnp.maximum(mid, 0.0)
        actbuf[s] = (mid * mid).astype(BF16)

    def down(k):
        s = slot_of(k)
        acc_ref[...] += jnp.dot(actbuf[s], w2buf[s], preferred_element_type=F32)

    rows = pl.ds(pl.multiple_of(i * TM_MLP, 16), TM_MLP)
    h_copy = pltpu.make_async_copy(h_hbm.at[rows, :], acc_ref, hsem.at[0])
    h_copy.start()

    @pl.when(i == 0)
    def _():
        w1_copy(0).start()
        w1_copy(1).start()
        w2_copy(0).start()

    w1_copy(0).wait()
    up(0)

    w1_copy(2).start()
    w2_copy(1).start()
    w1_copy(1).wait()
    up(1)
    h_copy.wait()
    w2_copy(0).wait()
    down(0)

    def step(f, carry):
        @pl.when(f + 2 < NF_MLP)
        def _():
            w1_copy(f + 2).start()

        @pl.when(jnp.logical_and(f + 2 == NF_MLP, more_tiles))
        def _():
            w1_copy(0).start()

        w2_copy(f + 1).start()
        w1_copy(f + 1).wait()
        w2_copy(f).wait()
        up(f + 1)
        down(f)
        return carry

    lax.fori_loop(1, NF_MLP - 1, step, 0)

    @pl.when(more_tiles)
    def _():
        w1_copy(1).start()
        w2_copy(0).start()

    w2_copy(NF_MLP - 1).wait()

    def last_down(rows):
        s = slot_of(NF_MLP - 1)
        acc_ref[rows, :] += jnp.dot(actbuf[s, rows, :], w2buf[s], preferred_element_type=F32)

    return last_down


def _mlp_kernel(h_hbm, xn_ref, w1_hbm, w2_hbm, g_ref, ho_ref, xo_ref, w1buf, w2buf, actbuf, wsem,
                hsem, *, layer):
    last_down = _mlp_accumulate(h_hbm, xn_ref, w1_hbm, w2_hbm, ho_ref, w1buf, w2buf, actbuf, wsem,
                                hsem, layer)
    for rows in MLP_FINISH_BLOCKS:
        last_down(rows)
        xo_ref[rows, :] = _rms_gain(ho_ref[rows, :], g_ref[...]).astype(xo_ref.dtype)


def _mlp_final_kernel(h_hbm, xn_ref, w1_hbm, w2_hbm, g_ref, out_hbm, w1buf, w2buf, actbuf, wsem,
                      hsem, ho_ref, obuf, osem, *, layer):
    last_down = _mlp_accumulate(h_hbm, xn_ref, w1_hbm, w2_hbm, ho_ref, w1buf, w2buf, actbuf, wsem,
                                hsem, layer)
    i = pl.program_id(0)
    tiles_per_seq = SEQ_LEN // TM_MLP
    b = i // tiles_per_seq
    ti = i % tiles_per_seq
    body_rows = TM_MLP - N_META
    body = pltpu.make_async_copy(
        obuf.at[pl.ds(N_META, body_rows), :],
        out_hbm.at[b, pl.ds(pl.multiple_of(ti * TM_MLP, 16), body_rows), :], osem.at[0])
    head_start = pl.multiple_of(jnp.maximum(ti * TM_MLP - N_META, 0), 16)
    head = pltpu.make_async_copy(
        obuf.at[pl.ds(0, N_META), :], out_hbm.at[b, pl.ds(head_start, N_META), :], osem.at[1])

    @pl.when(i > 0)
    def _():
        body.wait()

    @pl.when(jnp.logical_and(i > 0, (i - 1) % tiles_per_seq > 0))
    def _():
        head.wait()

    for rows in MLP_FINISH_BLOCKS:
        last_down(rows)
        obuf[rows, :] = _rms_gain(ho_ref[rows, :], g_ref[...])
    body.start()

    @pl.when(ti > 0)
    def _():
        head.start()

    @pl.when(i == pl.num_programs(0) - 1)
    def _():
        body.wait()
        head.wait()


def _mlp_call(h, xn, w1, w2, g, layer, final):
    row = lambda i: (i, 0)
    scratch = [pltpu.VMEM((2, D_MODEL, TF_MLP), BF16),
               pltpu.VMEM((2, TF_MLP, D_MODEL), BF16),
               pltpu.VMEM((2, TM_MLP, TF_MLP), BF16),
               pltpu.SemaphoreType.DMA((2, 2)),
               pltpu.SemaphoreType.DMA((1,))]
    if final:
        body = _mlp_final_kernel
        out_specs = pl.BlockSpec(memory_space=pl.ANY)
        out_shape = jax.ShapeDtypeStruct((BATCH, SEQ, D_MODEL), F32)
        scratch += [pltpu.VMEM((TM_MLP, D_MODEL), F32), pltpu.VMEM((TM_MLP, D_MODEL), F32),
                    pltpu.SemaphoreType.DMA((2,))]
    else:
        body = _mlp_kernel
        out_specs = [pl.BlockSpec((TM_MLP, D_MODEL), row), pl.BlockSpec((TM_MLP, D_MODEL), row)]
        out_shape = [jax.ShapeDtypeStruct((ROWS, D_MODEL), F32),
                     jax.ShapeDtypeStruct((ROWS, D_MODEL), BF16)]
    return pl.pallas_call(
        functools.partial(body, layer=layer),
        grid=(ROWS // TM_MLP,),
        in_specs=[pl.BlockSpec(memory_space=pl.ANY),
                  pl.BlockSpec((TM_MLP, D_MODEL), row),
                  pl.BlockSpec(memory_space=pl.ANY),
                  pl.BlockSpec(memory_space=pl.ANY),
                  pl.BlockSpec((1, D_MODEL), lambda i: (0, 0))],
        out_specs=out_specs,
        out_shape=out_shape,
        scratch_shapes=scratch,
        compiler_params=_params(("arbitrary",)),
        name="sq_relu_mlp_final" if final else "sq_relu_mlp",
    )(h, xn, w1, w2, g)


def _channel_dft_table():
    c = jnp.arange(GROUP_DIM, dtype=jnp.int32)
    ang = ((c[:, None] * c[None, :]) % GROUP_DIM).astype(F32) * (2.0 * math.pi / GROUP_DIM)
    return jnp.concatenate([jnp.cos(ang), jnp.sin(ang)], axis=1).astype(BF16)


def _sequence_dft_tables():
    m = jnp.arange(DFT_LEN, dtype=jnp.int32)[:, None]
    k = jnp.arange(DFT_K, dtype=jnp.int32)[None, :]
    ang = ((m * k) % DFT_LEN).astype(F32) * (2.0 * math.pi / DFT_LEN)
    live = k < DFT_LEN
    cm = jnp.where(live, jnp.cos(ang), 0.0).astype(BF16)
    sm = jnp.where(live, jnp.sin(ang), 0.0).astype(BF16)
    return cm, sm


def _twiddle_table():
    k = jnp.arange(DFT_K, dtype=jnp.int32)
    cols = []
    for r in range(1, RADIX):
        ang = (r * k).astype(F32) * (2.0 * math.pi / SEQ_LEN)
        cols += [jnp.cos(ang), jnp.sin(ang)]
    tw = jnp.stack(cols, axis=1)
    return jnp.where((k < DFT_LEN)[:, None], tw, 0.0)


def _interleave_permutation():
    j = jnp.arange(TM_OUT, dtype=jnp.int32)[:, None]
    c = jnp.arange(TM_OUT, dtype=jnp.int32)[None, :]
    src = (j % RADIX) * DFT_ROWS_PER_TILE + j // RADIX
    return (c == src).astype(BF16)


def kernel(x, meta_tokens, norm_mix_g, norm_mlp_g, norm_final_g, ab_w_in, ab_w_out, ab_conv_w,
           ab_conv_b, ab_ln_g, ab_ln_b, c_w_in, c_conv_w, c_w_out, mlp_w1, mlp_w2):
    depth = norm_mix_g.shape[0]
    cs = _channel_dft_table()
    cm, sm = _sequence_dft_tables()
    tw = _twiddle_table()
    perm = _interleave_permutation()

    ab_w_in, ab_w_out, c_w_in, c_w_out, mlp_w1, mlp_w2 = (
        w.astype(BF16) for w in (ab_w_in, ab_w_out, c_w_in, c_w_out, mlp_w1, mlp_w2))

    h, xn = _prologue_call(x, meta_tokens.astype(x.dtype), norm_mix_g[0][None, :])
    for layer in range(depth):
        i = layer // 2
        g_mlp = norm_mlp_g[layer][None, :]
        if layer % 2 == 0:
            uc, us, glu = _even_in_call(xn, ab_w_in, cs, i)
            shape5 = (GROUPS, BATCH, RADIX, DFT_LEN, GROUP_DIM)
            p, q = _butterfly_call(uc.reshape(shape5), us.reshape(shape5), tw)
            ya = _dft_call(cm, sm, p, q)
            cw = jnp.broadcast_to(ab_conv_w[i][:, None, :], (CONV_KERNEL, SUBLANES, MIX_WIDTH))
            h, xn = _even_out_call(h, ya, glu, cw, ab_conv_b[i][None, :], ab_ln_g[i][None, :],
                                   ab_ln_b[i][None, :], perm, ab_w_out, g_mlp, i)
        else:
            gb, p = _odd_in_call(xn, c_w_in, i)
            h, xn = _odd_out_call(h, gb, p, c_conv_w[i], c_w_out, g_mlp, i)
        final = layer == depth - 1
        g_next = norm_final_g if final else norm_mix_g[layer + 1]
        if final:
            return _mlp_call(h, xn, mlp_w1, mlp_w2, g_next[None, :], layer, True)
        h, xn = _mlp_call(h, xn, mlp_w1, mlp_w2, g_next[None, :], layer, False)
```

```python
import functools
import math

import jax
import jax.numpy as jnp
from jax import lax
from jax.experimental import pallas as pl
from jax.experimental.pallas import tpu as pltpu

F32 = jnp.float32
BF16 = jnp.bfloat16

D_MODEL = 2048
BATCH = 2
SEQ = 8192
N_META = 16
SEQ_LEN = SEQ + N_META
ROWS = BATCH * SEQ_LEN
GROUPS = 4
GROUP_DIM = 256
MIX_WIDTH = GROUPS * GROUP_DIM
CONV_KERNEL = 31
CONV_PAD = (CONV_KERNEL - 1) // 2
D_FF = 4 * D_MODEL
EPS = 1e-6

RADIX = 9
DFT_LEN = SEQ_LEN // RADIX
DFT_K = 1024
DFT_NSPLIT = 3
TW_COLS = 2 * (RADIX - 1)

TM_PRO = 912
TM_IN = 1824
TM_OUT = 432
TM_MLP = 912
TF_MLP = 1024
NF_MLP = D_FF // TF_MLP
MLP_FINISH_BLOCKS = (slice(0, 448), slice(448, TM_MLP))
ODD_CHUNK = 512
HALO = 16
SUBLANES = 8
CONV_RC = 48
SHIFT_ROWS = TM_OUT + 2 * HALO - SUBLANES
TILES_PER_SEQ = SEQ_LEN // TM_OUT
DFT_ROWS_PER_TILE = TM_OUT // RADIX
VMEM_LIMIT = 58 * 1024 * 1024

assert SEQ_LEN % TM_PRO == 0 and SEQ_LEN % TM_OUT == 0 and SEQ_LEN % TM_MLP == 0
assert ROWS % TM_IN == 0 and TM_IN % 16 == 0
assert TM_OUT % CONV_RC == 0 and TM_OUT % HALO == 0 and CONV_RC % 16 == 0
assert DFT_LEN % 16 == 0 and DFT_ROWS_PER_TILE % 16 == 0 and TM_OUT % RADIX == 0
assert DFT_K % 256 == 0 and RADIX % DFT_NSPLIT == 0 and NF_MLP % 2 == 0


def _params(sem):
    return pltpu.CompilerParams(dimension_semantics=sem, vmem_limit_bytes=VMEM_LIMIT)


WEIGHT_DMA_PRIORITY = 1


def _rms_gain(x, g):
    ms = jnp.mean(x * x, axis=-1, keepdims=True)
    return x * lax.rsqrt(ms + EPS) * g


def _resident(shape):
    nd = len(shape)
    return pl.BlockSpec(shape, lambda *_: (0,) * nd, pipeline_mode=pl.Buffered(1))


def _resident_layer(shape, layer):
    nd = len(shape)
    return pl.BlockSpec((None,) + shape, lambda *_: (layer,) + (0,) * nd,
                        pipeline_mode=pl.Buffered(1))


def _prologue_kernel(x_hbm, meta_ref, g_ref, h_ref, xn_ref, xbuf, sem):
    i = pl.program_id(0)
    tiles_per_seq = SEQ_LEN // TM_PRO
    body_rows = TM_PRO - N_META

    def copies(t, slot):
        b = t // tiles_per_seq
        ti = t % tiles_per_seq
        body = pltpu.make_async_copy(
            x_hbm.at[b, pl.ds(pl.multiple_of(ti * TM_PRO, 16), body_rows), :],
            xbuf.at[slot, pl.ds(N_META, body_rows), :], sem.at[0, slot])
        head_start = pl.multiple_of(jnp.maximum(ti * TM_PRO - N_META, 0), 16)
        head = pltpu.make_async_copy(
            x_hbm.at[b, pl.ds(head_start, N_META), :],
            xbuf.at[slot, pl.ds(0, N_META), :], sem.at[1, slot])
        return body, head, ti

    def start(t, slot):
        body, head, ti = copies(t, slot)
        body.start()

        @pl.when(ti > 0)
        def _():
            head.start()

    slot = lax.rem(i, 2)

    @pl.when(i == 0)
    def _():
        start(0, 0)

    @pl.when(i + 1 < pl.num_programs(0))
    def _():
        start(i + 1, 1 - slot)

    body, head, ti = copies(i, slot)
    body.wait()

    @pl.when(ti > 0)
    def _():
        head.wait()

    @pl.when(ti == 0)
    def _():
        xbuf[slot, 0:N_META, :] = meta_ref[...]

    hval = xbuf[slot]
    h_ref[...] = hval
    xn_ref[...] = _rms_gain(hval, g_ref[...]).astype(xn_ref.dtype)


def _prologue_call(x, meta, g):
    row = lambda i: (i, 0)
    return pl.pallas_call(
        _prologue_kernel,
        grid=(ROWS // TM_PRO,),
        in_specs=[pl.BlockSpec(memory_space=pl.ANY),
                  pl.BlockSpec((N_META, D_MODEL), lambda i: (0, 0)),
                  pl.BlockSpec((1, D_MODEL), lambda i: (0, 0))],
        out_specs=[pl.BlockSpec((TM_PRO, D_MODEL), row), pl.BlockSpec((TM_PRO, D_MODEL), row)],
        out_shape=[jax.ShapeDtypeStruct((ROWS, D_MODEL), F32),
                   jax.ShapeDtypeStruct((ROWS, D_MODEL), BF16)],
        scratch_shapes=[pltpu.VMEM((2, TM_PRO, D_MODEL), F32), pltpu.SemaphoreType.DMA((2, 2))],
        compiler_params=_params(("arbitrary",)),
        name="prologue_concat_norm",
    )(x, meta, g)


def _even_in_kernel(xn_ref, wu_ref, wa_ref, wg_ref, cs_ref, uc_ref, us_ref, glu_ref, wcat):
    wcat[:, 0:GROUP_DIM] = wu_ref[...]
    wcat[:, GROUP_DIM:2 * GROUP_DIM] = wa_ref[...]
    wcat[:, 2 * GROUP_DIM:] = wg_ref[...]
    proj = jnp.dot(xn_ref[...], wcat[...], preferred_element_type=F32)
    u = proj[:, :GROUP_DIM].astype(BF16)
    cs = jnp.dot(u, cs_ref[...], preferred_element_type=F32)
    uc_ref[...] = cs[:, :GROUP_DIM].astype(BF16)
    us_ref[...] = cs[:, GROUP_DIM:].astype(BF16)
    a = proj[:, GROUP_DIM:2 * GROUP_DIM]
    gate = proj[:, 2 * GROUP_DIM:]
    glu_ref[...] = a * jax.nn.sigmoid(gate)


def _even_in_call(xn, w, cs, layer):
    grp = jax.ShapeDtypeStruct((GROUPS, ROWS, GROUP_DIM), BF16)
    wspec = lambda part: pl.BlockSpec((None, D_MODEL, GROUP_DIM),
                                      lambda i, g: (layer, 0, part * GROUPS + g))
    return pl.pallas_call(
        _even_in_kernel,
        grid=(ROWS // TM_IN, GROUPS),
        in_specs=[pl.BlockSpec((TM_IN, D_MODEL), lambda i, g: (i, 0)),
                  wspec(0), wspec(1), wspec(2),
                  pl.BlockSpec((GROUP_DIM, 2 * GROUP_DIM), lambda i, g: (0, 0))],
        out_specs=[pl.BlockSpec((None, TM_IN, GROUP_DIM), lambda i, g: (g, i, 0)),
                   pl.BlockSpec((None, TM_IN, GROUP_DIM), lambda i, g: (g, i, 0)),
                   pl.BlockSpec((TM_IN, GROUP_DIM), lambda i, g: (i, g))],
        out_shape=[grp, grp, jax.ShapeDtypeStruct((ROWS, MIX_WIDTH), F32)],
        scratch_shapes=[pltpu.VMEM((D_MODEL, 3 * GROUP_DIM), BF16)],
        compiler_params=_params(("parallel", "arbitrary")),
        name="even_in_proj",
    )(xn, w, w, w, cs)


def _dft3(z0, z1, z2):
    half_sqrt3 = 0.5 * math.sqrt(3.0)
    sx, sy = z1[0] + z2[0], z1[1] + z2[1]
    dx, dy = z1[0] - z2[0], z1[1] - z2[1]
    mx, my = z0[0] - 0.5 * sx, z0[1] - 0.5 * sy
    return ((z0[0] + sx, z0[1] + sy),
            (mx + half_sqrt3 * dy, my - half_sqrt3 * dx),
            (mx - half_sqrt3 * dy, my + half_sqrt3 * dx))


def _butterfly_kernel(uc_ref, us_ref, tw_ref, p_ref, q_ref):
    lane_blocks = GROUP_DIM // 128

    def chunk(c, carry):
        r0 = pl.multiple_of(c * 16, 16)
        rows = pl.ds(r0, 16)
        tw = tw_ref[rows, :]
        for lb in range(lane_blocks):
            lanes = slice(lb * 128, (lb + 1) * 128)
            z = [(uc_ref[k, rows, lanes].astype(F32), -us_ref[k, rows, lanes].astype(F32))
                 for k in range(RADIX)]
            inner = []
            for q0 in range(3):
                outs = _dft3(z[q0], z[q0 + 3], z[q0 + 6])
                row = []
                for r0_ in range(3):
                    x, y = outs[r0_]
                    if r0_ * q0:
                        ang = -2.0 * math.pi * r0_ * q0 / 9.0
                        ct, st = math.cos(ang), math.sin(ang)
                        x, y = ct * x - st * y, ct * y + st * x
                    row.append((x, y))
                inner.append(row)
            for r0_ in range(3):
                outs = _dft3(inner[0][r0_], inner[1][r0_], inner[2][r0_])
                for r1 in range(3):
                    r = r0_ + 3 * r1
                    x, y = outs[r1]
                    if r == 0:
                        p, q = x, y
                    else:
                        cr = tw[:, 2 * r - 2:2 * r - 1]
                        sr = tw[:, 2 * r - 1:2 * r]
                        p = cr * x + sr * y
                        q = cr * y - sr * x
                    out_lanes = slice(r * GROUP_DIM + lb * 128, r * GROUP_DIM + (lb + 1) * 128)
                    p_ref[rows, out_lanes] = p.astype(BF16)
                    q_ref[rows, out_lanes] = q.astype(BF16)
        return carry

    lax.fori_loop(0, DFT_LEN // 16, chunk, 0, unroll=3)
    pad = jnp.zeros((DFT_K - DFT_LEN, RADIX * GROUP_DIM), BF16)
    p_ref[DFT_LEN:, :] = pad
    q_ref[DFT_LEN:, :] = pad


def _butterfly_call(uc, us, tw):
    in_spec = pl.BlockSpec((None, None, RADIX, DFT_K, GROUP_DIM), lambda g, b: (g, b, 0, 0, 0))
    out_spec = pl.BlockSpec((None, None, DFT_K, RADIX * GROUP_DIM), lambda g, b: (g, b, 0, 0))
    out = jax.ShapeDtypeStruct((GROUPS, BATCH, DFT_K, RADIX * GROUP_DIM), BF16)
    return pl.pallas_call(
        _butterfly_kernel,
        grid=(GROUPS, BATCH),
        in_specs=[in_spec, in_spec, pl.BlockSpec((DFT_K, TW_COLS), lambda g, b: (0, 0))],
        out_specs=[out_spec, out_spec],
        out_shape=[out, out],
        compiler_params=_params(("parallel", "parallel")),
        name="dft_butterfly",
    )(uc, us, tw)


def _dft_kernel(c_ref, s_ref, p_ref, q_ref, o_ref, *, scale):
    acc = jnp.dot(c_ref[...], p_ref[...], preferred_element_type=F32)
    acc = acc + jnp.dot(s_ref[...], q_ref[...], preferred_element_type=F32)
    for r in range(DFT_NSPLIT):
        o_ref[r] = (acc[:, r * GROUP_DIM:(r + 1) * GROUP_DIM] * scale).astype(o_ref.dtype)


def _dft_call(cm, sm, p, q):
    width = DFT_NSPLIT * GROUP_DIM
    in_spec = pl.BlockSpec((None, None, DFT_K, width), lambda g, b, j: (g, b, 0, j))
    return pl.pallas_call(
        functools.partial(_dft_kernel, scale=1.0 / math.sqrt(SEQ_LEN * GROUP_DIM)),
        grid=(GROUPS, BATCH, RADIX // DFT_NSPLIT),
        in_specs=[_resident((DFT_LEN, DFT_K)), _resident((DFT_LEN, DFT_K)), in_spec, in_spec],
        out_specs=pl.BlockSpec((None, None, DFT_NSPLIT, DFT_LEN, GROUP_DIM),
                               lambda g, b, j: (g, b, j, 0, 0)),
        out_shape=jax.ShapeDtypeStruct((GROUPS, BATCH, RADIX, DFT_LEN, GROUP_DIM), BF16),
        compiler_params=_params(("parallel", "parallel", "parallel")),
        name="dft_matmul",
    )(cm, sm, p, q)


def _even_out_kernel(h_ref, ya_ref, xm_ref, xp_ref, xnx_ref, cw_ref, cb_ref, lg_ref, lb_ref,
                     perm_ref, w_ref, g_ref, ho_ref, xo_ref, xpad, xshift, yperm, ycat):
    i = pl.program_id(0)
    first = (i % TILES_PER_SEQ) == 0
    last = (i % TILES_PER_SEQ) == TILES_PER_SEQ - 1
    xpad[0:HALO, :] = jnp.where(first, 0.0, xp_ref[...])
    xpad[HALO:HALO + TM_OUT, :] = xm_ref[...]
    xpad[HALO + TM_OUT:, :] = jnp.where(last, 0.0, xnx_ref[...])

    for g in range(GROUPS):
        for r in range(RADIX):
            yperm[r * DFT_ROWS_PER_TILE:(r + 1) * DFT_ROWS_PER_TILE,
                  g * GROUP_DIM:(g + 1) * GROUP_DIM] = ya_ref[g, r]
    ycat[:, :MIX_WIDTH] = jnp.dot(perm_ref[...], yperm[...],
                                  preferred_element_type=F32).astype(BF16)
    ho_ref[...] = h_ref[...] + jnp.dot(ycat[:, :MIX_WIDTH], w_ref[:MIX_WIDTH, :],
                                       preferred_element_type=F32)

    row_vregs = CONV_RC // SUBLANES
    for g in range(GROUPS):
        lanes = slice(g * GROUP_DIM, (g + 1) * GROUP_DIM)
        out_lanes = slice(MIX_WIDTH + g * GROUP_DIM, MIX_WIDTH + (g + 1) * GROUP_DIM)
        for s in range(SUBLANES):
            xshift[g, s] = xpad[s:s + SHIFT_ROWS, lanes]
        for c in range(TM_OUT // CONV_RC):
            r0 = c * CONV_RC
            acc = [None] * row_vregs
            for j in range(CONV_KERNEL):
                oq, os_ = divmod(HALO - CONV_PAD + j, SUBLANES)
                w = cw_ref[j, :, lanes]
                for r in range(row_vregs):
                    start = r0 + SUBLANES * (oq + r)
                    term = w * xshift[g, os_, start:start + SUBLANES, :]
                    acc[r] = term if j == 0 else acc[r] + term
            conv = jnp.concatenate(acc, axis=0) + cb_ref[:, lanes]
            mu = jnp.mean(conv, axis=-1, keepdims=True)
            dev = conv - mu
            var = jnp.mean(dev * dev, axis=-1, keepdims=True)
            y = dev * lax.rsqrt(var + EPS) * lg_ref[:, lanes] + lb_ref[:, lanes]
            y = y * jax.nn.sigmoid(y)
            ycat[r0:r0 + CONV_RC, out_lanes] = y.astype(BF16)
        ho_ref[...] += jnp.dot(ycat[:, out_lanes], w_ref[out_lanes, :],
                               preferred_element_type=F32)

    xo_ref[...] = _rms_gain(ho_ref[...], g_ref[...]).astype(xo_ref.dtype)


def _even_out_call(h, ya, glu, cw, cb, lg, lb, perm, w, g, layer):
    hb = TM_OUT // HALO
    n_halo = ROWS // HALO
    row = lambda i: (i, 0)
    const = lambda i: (0, 0)
    return pl.pallas_call(
        _even_out_kernel,
        grid=(ROWS // TM_OUT,),
        in_specs=[pl.BlockSpec((TM_OUT, D_MODEL), row),
                  pl.BlockSpec((GROUPS, None, RADIX, DFT_ROWS_PER_TILE, GROUP_DIM),
                               lambda i: (0, i // TILES_PER_SEQ, 0, i % TILES_PER_SEQ, 0)),
                  pl.BlockSpec((TM_OUT, MIX_WIDTH), row),
                  pl.BlockSpec((HALO, MIX_WIDTH), lambda i: (jnp.maximum(i * hb - 1, 0), 0)),
                  pl.BlockSpec((HALO, MIX_WIDTH),
                               lambda i: (jnp.minimum((i + 1) * hb, n_halo - 1), 0)),
                  pl.BlockSpec((CONV_KERNEL, SUBLANES, MIX_WIDTH), lambda i: (0, 0, 0)),
                  pl.BlockSpec((1, MIX_WIDTH), const),
                  pl.BlockSpec((1, MIX_WIDTH), const),
                  pl.BlockSpec((1, MIX_WIDTH), const),
                  pl.BlockSpec((TM_OUT, TM_OUT), const),
                  _resident_layer((D_MODEL, D_MODEL), layer),
                  pl.BlockSpec((1, D_MODEL), const)],
        out_specs=[pl.BlockSpec((TM_OUT, D_MODEL), row), pl.BlockSpec((TM_OUT, D_MODEL), row)],
        out_shape=[jax.ShapeDtypeStruct((ROWS, D_MODEL), F32),
                   jax.ShapeDtypeStruct((ROWS, D_MODEL), BF16)],
        scratch_shapes=[pltpu.VMEM((TM_OUT + 2 * HALO, MIX_WIDTH), F32),
                        pltpu.VMEM((GROUPS, SUBLANES, SHIFT_ROWS, GROUP_DIM), F32),
                        pltpu.VMEM((TM_OUT, MIX_WIDTH), BF16),
                        pltpu.VMEM((TM_OUT, D_MODEL), BF16)],
        compiler_params=_params(("parallel",)),
        name="even_out_proj",
    )(h, ya, glu, glu, glu, cw, cb, lg, lb, perm, w, g)


def _odd_in_kernel(xn_ref, wb_ref, wc_ref, wh_ref, gb_ref, p_ref):
    xn = xn_ref[...]
    gb_ref[...] = jnp.dot(xn, wb_ref[...], preferred_element_type=F32).astype(gb_ref.dtype)
    gc = jnp.dot(xn, wc_ref[...], preferred_element_type=F32)
    hin = jnp.dot(xn, wh_ref[...], preferred_element_type=F32)
    p_ref[...] = (gc * hin).astype(p_ref.dtype)


def _odd_in_call(xn, w, layer):
    n_chunks = D_MODEL // ODD_CHUNK
    out = jax.ShapeDtypeStruct((ROWS, D_MODEL), BF16)
    wspec = lambda part: pl.BlockSpec((None, D_MODEL, ODD_CHUNK),
                                      lambda i, n: (layer, 0, part * n_chunks + n))
    return pl.pallas_call(
        _odd_in_kernel,
        grid=(ROWS // TM_IN, n_chunks),
        in_specs=[pl.BlockSpec((TM_IN, D_MODEL), lambda i, n: (i, 0)),
                  wspec(0), wspec(1), wspec(2)],
        out_specs=[pl.BlockSpec((TM_IN, ODD_CHUNK), lambda i, n: (i, n)),
                   pl.BlockSpec((TM_IN, ODD_CHUNK), lambda i, n: (i, n))],
        out_shape=[out, out],
        compiler_params=_params(("parallel", "arbitrary")),
        name="odd_in_proj",
    )(xn, w, w, w)


def _odd_out_kernel(h_ref, gb_ref, pm_ref, pp_ref, pn_ref, cw_ref, w_ref, g_ref,
                    ho_ref, xo_ref, xmid, xprev, xnext, ybuf):
    i = pl.program_id(0)
    first = (i % TILES_PER_SEQ) == 0
    last = (i % TILES_PER_SEQ) == TILES_PER_SEQ - 1
    xmid[...] = pm_ref[...].astype(F32)
    xprev[0:1, :] = jnp.where(first, 0.0, pp_ref[HALO - 1:HALO, :].astype(F32))
    xprev[1:TM_OUT, :] = xmid[0:TM_OUT - 1, :]
    xnext[0:TM_OUT - 1, :] = xmid[1:TM_OUT, :]
    xnext[TM_OUT - 1:TM_OUT, :] = jnp.where(last, 0.0, pn_ref[0:1, :].astype(F32))

    n_blocks = 4
    width = D_MODEL // n_blocks
    for blk in range(n_blocks):
        lanes = slice(blk * width, (blk + 1) * width)
        for c in range(TM_OUT // 16):
            rows = slice(c * 16, (c + 1) * 16)
            conv = cw_ref[0:1, lanes] * xprev[rows, lanes]
            conv = conv + cw_ref[1:2, lanes] * xmid[rows, lanes]
            conv = conv + cw_ref[2:3, lanes] * xnext[rows, lanes]
            ybuf[rows, lanes] = (gb_ref[rows, lanes].astype(F32) * conv).astype(BF16)
        part = jnp.dot(ybuf[:, lanes], w_ref[lanes, :], preferred_element_type=F32)
        if blk == 0:
            ho_ref[...] = h_ref[...] + part
        else:
            ho_ref[...] += part

    xo_ref[...] = _rms_gain(ho_ref[...], g_ref[...]).astype(xo_ref.dtype)


def _odd_out_call(h, gb, p, cw, w, g, layer):
    hb = TM_OUT // HALO
    n_halo = ROWS // HALO
    row = lambda i: (i, 0)
    const = lambda i: (0, 0)
    return pl.pallas_call(
        _odd_out_kernel,
        grid=(ROWS // TM_OUT,),
        in_specs=[pl.BlockSpec((TM_OUT, D_MODEL), row),
                  pl.BlockSpec((TM_OUT, D_MODEL), row),
                  pl.BlockSpec((TM_OUT, D_MODEL), row),
                  pl.BlockSpec((HALO, D_MODEL), lambda i: (jnp.maximum(i * hb - 1, 0), 0)),
                  pl.BlockSpec((HALO, D_MODEL),
                               lambda i: (jnp.minimum((i + 1) * hb, n_halo - 1), 0)),
                  pl.BlockSpec((3, D_MODEL), const),
                  _resident_layer((D_MODEL, D_MODEL), layer),
                  pl.BlockSpec((1, D_MODEL), const)],
        out_specs=[pl.BlockSpec((TM_OUT, D_MODEL), row), pl.BlockSpec((TM_OUT, D_MODEL), row)],
        out_shape=[jax.ShapeDtypeStruct((ROWS, D_MODEL), F32),
                   jax.ShapeDtypeStruct((ROWS, D_MODEL), BF16)],
        scratch_shapes=[pltpu.VMEM((TM_OUT, D_MODEL), F32),
                        pltpu.VMEM((TM_OUT, D_MODEL), F32),
                        pltpu.VMEM((TM_OUT, D_MODEL), F32),
                        pltpu.VMEM((TM_OUT, D_MODEL), BF16)],
        compiler_params=_params(("parallel",)),
        name="odd_out_proj",
    )(h, gb, p, p, p, cw, w, g)


def _mlp_accumulate(h_hbm, xn_ref, w1_hbm, w2_hbm, acc_ref, w1buf, w2buf, actbuf, wsem, hsem,
                    layer):
    i = pl.program_id(0)
    more_tiles = i + 1 < pl.num_programs(0)

    def slot_of(k):
        return k % 2 if isinstance(k, int) else lax.rem(k, 2)

    def chunk_of(k):
        start = k * TF_MLP if isinstance(k, int) else pl.multiple_of(k * TF_MLP, TF_MLP)
        return pl.ds(start, TF_MLP)

    def w1_copy(k):
        s = slot_of(k)
        return pltpu.make_async_copy(w1_hbm.at[layer, :, chunk_of(k)], w1buf.at[s], wsem.at[0, s])

    def w2_copy(k):
        s = slot_of(k)
        return pltpu.make_async_copy(w2_hbm.at[layer, chunk_of(k), :], w2buf.at[s], wsem.at[1, s])

    def up(k):
        s = slot_of(k)
        mid = jnp.dot(xn_ref[...], w1buf[s], preferred_element_type=F32)
        mid = jnp.maximum(mid, 0.0)
        actbuf[s] = (mid * mid).astype(BF16)

    def down(k):
        s = slot_of(k)
        acc_ref[...] += jnp.dot(actbuf[s], w2buf[s], preferred_element_type=F32)

    rows = pl.ds(pl.multiple_of(i * TM_MLP, 16), TM_MLP)
    h_copy = pltpu.make_async_copy(h_hbm.at[rows, :], acc_ref, hsem.at[0])
    h_copy.start()

    @pl.when(i == 0)
    def _():
        w1_copy(0).start(priority=WEIGHT_DMA_PRIORITY)
        w1_copy(1).start(priority=WEIGHT_DMA_PRIORITY)
        w2_copy(0).start(priority=WEIGHT_DMA_PRIORITY)

    w1_copy(0).wait()
    up(0)

    w1_copy(2).start(priority=WEIGHT_DMA_PRIORITY)
    w2_copy(1).start(priority=WEIGHT_DMA_PRIORITY)
    w1_copy(1).wait()
    up(1)
    h_copy.wait()
    w2_copy(0).wait()
    down(0)

    def step(f, carry):
        @pl.when(f + 2 < NF_MLP)
        def _():
            w1_copy(f + 2).start(priority=WEIGHT_DMA_PRIORITY)

        @pl.when(jnp.logical_and(f + 2 == NF_MLP, more_tiles))
        def _():
            w1_copy(0).start(priority=WEIGHT_DMA_PRIORITY)

        w2_copy(f + 1).start(priority=WEIGHT_DMA_PRIORITY)
        w1_copy(f + 1).wait()
        w2_copy(f).wait()
        up(f + 1)
        down(f)
        return carry

    lax.fori_loop(1, NF_MLP - 1, step, 0)

    @pl.when(more_tiles)
    def _():
        w1_copy(1).start(priority=WEIGHT_DMA_PRIORITY)
        w2_copy(0).start(priority=WEIGHT_DMA_PRIORITY)

    w2_copy(NF_MLP - 1).wait()

    def last_down(rows):
        s = slot_of(NF_MLP - 1)
        acc_ref[rows, :] += jnp.dot(actbuf[s, rows, :], w2buf[s], preferred_element_type=F32)

    return last_down


def _mlp_kernel(h_hbm, xn_ref, w1_hbm, w2_hbm, g_ref, ho_ref, xo_ref, w1buf, w2buf, actbuf, wsem,
                hsem, *, layer):
    last_down = _mlp_accumulate(h_hbm, xn_ref, w1_hbm, w2_hbm, ho_ref, w1buf, w2buf, actbuf, wsem,
                                hsem, layer)
    for rows in MLP_FINISH_BLOCKS:
        last_down(rows)
        xo_ref[rows, :] = _rms_gain(ho_ref[rows, :], g_ref[...]).astype(xo_ref.dtype)


def _mlp_final_kernel(h_hbm, xn_ref, w1_hbm, w2_hbm, g_ref, out_hbm, w1buf, w2buf, actbuf, wsem,
                      hsem, ho_ref, obuf, osem, *, layer):
    last_down = _mlp_accumulate(h_hbm, xn_ref, w1_hbm, w2_hbm, ho_ref, w1buf, w2buf, actbuf, wsem,
                                hsem, layer)
    i = pl.program_id(0)
    tiles_per_seq = SEQ_LEN // TM_MLP
    b = i // tiles_per_seq
    ti = i % tiles_per_seq
    body_rows = TM_MLP - N_META
    body = pltpu.make_async_copy(
        obuf.at[pl.ds(N_META, body_rows), :],
        out_hbm.at[b, pl.ds(pl.multiple_of(ti * TM_MLP, 16), body_rows), :], osem.at[0])
    head_start = pl.multiple_of(jnp.maximum(ti * TM_MLP - N_META, 0), 16)
    head = pltpu.make_async_copy(
        obuf.at[pl.ds(0, N_META), :], out_hbm.at[b, pl.ds(head_start, N_META), :], osem.at[1])

    @pl.when(i > 0)
    def _():
        body.wait()

    @pl.when(jnp.logical_and(i > 0, (i - 1) % tiles_per_seq > 0))
    def _():
        head.wait()

    for rows in MLP_FINISH_BLOCKS:
        last_down(rows)
        obuf[rows, :] = _rms_gain(ho_ref[rows, :], g_ref[...])
    body.start()

    @pl.when(ti > 0)
    def _():
        head.start()

    @pl.when(i == pl.num_programs(0) - 1)
    def _():
        body.wait()
        head.wait()


def _mlp_call(h, xn, w1, w2, g, layer, final):
    row = lambda i: (i, 0)
    scratch = [pltpu.VMEM((2, D_MODEL, TF_MLP), BF16),
               pltpu.VMEM((2, TF_MLP, D_MODEL), BF16),
               pltpu.VMEM((2, TM_MLP, TF_MLP), BF16),
               pltpu.SemaphoreType.DMA((2, 2)),
               pltpu.SemaphoreType.DMA((1,))]
    if final:
        body = _mlp_final_kernel
        out_specs = pl.BlockSpec(memory_space=pl.ANY)
        out_shape = jax.ShapeDtypeStruct((BATCH, SEQ, D_MODEL), F32)
        scratch += [pltpu.VMEM((TM_MLP, D_MODEL), F32), pltpu.VMEM((TM_MLP, D_MODEL), F32),
                    pltpu.SemaphoreType.DMA((2,))]
    else:
        body = _mlp_kernel
        out_specs = [pl.BlockSpec((TM_MLP, D_MODEL), row), pl.BlockSpec((TM_MLP, D_MODEL), row)]
        out_shape = [jax.ShapeDtypeStruct((ROWS, D_MODEL), F32),
                     jax.ShapeDtypeStruct((ROWS, D_MODEL), BF16)]
    return pl.pallas_call(
        functools.partial(body, layer=layer),
        grid=(ROWS // TM_MLP,),
        in_specs=[pl.BlockSpec(memory_space=pl.ANY),
                  pl.BlockSpec((TM_MLP, D_MODEL), row),
                  pl.BlockSpec(memory_space=pl.ANY),
                  pl.BlockSpec(memory_space=pl.ANY),
                  pl.BlockSpec((1, D_MODEL), lambda i: (0, 0))],
        out_specs=out_specs,
        out_shape=out_shape,
        scratch_shapes=scratch,
        compiler_params=_params(("arbitrary",)),
        name="sq_relu_mlp_final" if final else "sq_relu_mlp",
    )(h, xn, w1, w2, g)


def _channel_dft_table():
    c = jnp.arange(GROUP_DIM, dtype=jnp.int32)
    ang = ((c[:, None] * c[None, :]) % GROUP_DIM).astype(F32) * (2.0 * math.pi / GROUP_DIM)
    return jnp.concatenate([jnp.cos(ang), jnp.sin(ang)], axis=1).astype(BF16)


def _sequence_dft_tables():
    m = jnp.arange(DFT_LEN, dtype=jnp.int32)[:, None]
    k = jnp.arange(DFT_K, dtype=jnp.int32)[None, :]
    ang = ((m * k) % DFT_LEN).astype(F32) * (2.0 * math.pi / DFT_LEN)
    live = k < DFT_LEN
    cm = jnp.where(live, jnp.cos(ang), 0.0).astype(BF16)
    sm = jnp.where(live, jnp.sin(ang), 0.0).astype(BF16)
    return cm, sm


def _twiddle_table():
    k = jnp.arange(DFT_K, dtype=jnp.int32)
    cols = []
    for r in range(1, RADIX):
        ang = (r * k).astype(F32) * (2.0 * math.pi / SEQ_LEN)
        cols += [jnp.cos(ang), jnp.sin(ang)]
    tw = jnp.stack(cols, axis=1)
    return jnp.where((k < DFT_LEN)[:, None], tw, 0.0)


def _interleave_permutation():
    j = jnp.arange(TM_OUT, dtype=jnp.int32)[:, None]
    c = jnp.arange(TM_OUT, dtype=jnp.int32)[None, :]
    src = (j % RADIX) * DFT_ROWS_PER_TILE + j // RADIX
    return (c == src).astype(BF16)


def kernel(x, meta_tokens, norm_mix_g, norm_mlp_g, norm_final_g, ab_w_in, ab_w_out, ab_conv_w,
           ab_conv_b, ab_ln_g, ab_ln_b, c_w_in, c_conv_w, c_w_out, mlp_w1, mlp_w2):
    depth = norm_mix_g.shape[0]
    cs = _channel_dft_table()
    cm, sm = _sequence_dft_tables()
    tw = _twiddle_table()
    perm = _interleave_permutation()

    ab_w_in, ab_w_out, c_w_in, c_w_out, mlp_w1, mlp_w2 = (
        w.astype(BF16) for w in (ab_w_in, ab_w_out, c_w_in, c_w_out, mlp_w1, mlp_w2))

    h, xn = _prologue_call(x, meta_tokens.astype(x.dtype), norm_mix_g[0][None, :])
    for layer in range(depth):
        i = layer // 2
        g_mlp = norm_mlp_g[layer][None, :]
        if layer % 2 == 0:
            uc, us, glu = _even_in_call(xn, ab_w_in, cs, i)
            shape5 = (GROUPS, BATCH, RADIX, DFT_LEN, GROUP_DIM)
            p, q = _butterfly_call(uc.reshape(shape5), us.reshape(shape5), tw)
            ya = _dft_call(cm, sm, p, q)
            cw = jnp.broadcast_to(ab_conv_w[i][:, None, :], (CONV_KERNEL, SUBLANES, MIX_WIDTH))
            h, xn = _even_out_call(h, ya, glu, cw, ab_conv_b[i][None, :], ab_ln_g[i][None, :],
                                   ab_ln_b[i][None, :], perm, ab_w_out, g_mlp, i)
        else:
            gb, p = _odd_in_call(xn, c_w_in, i)
            h, xn = _odd_out_call(h, gb, p, c_conv_w[i], c_w_out, g_mlp, i)
        final = layer == depth - 1
        g_next = norm_final_g if final else norm_mix_g[layer + 1]
        if final:
            return _mlp_call(h, xn, mlp_w1, mlp_w2, g_next[None, :], layer, True)
        h, xn = _mlp_call(h, xn, mlp_w1, mlp_w2, g_next[None, :], layer, False)
```
